```python
import jax, jax.numpy as jnp
from jax import lax
import numpy as np

D_MODEL = 2048
BATCH = 4
SEQ = 2048
DEPTH = 4
DEC_BATCH = 8
DEC_SEQ = 8
PAST_LEN = 16384
PAGE_SIZE = 128

N_MIXERS = 2
N_A_LAYERS = (DEPTH + 1) // 2
N_B_LAYERS = DEPTH // 2
SB_HEADS = 16
SB_HEAD_DIM = D_MODEL // SB_HEADS
SB_WIDTH = SB_HEADS * SB_HEAD_DIM
SB_SCALE = SB_HEAD_DIM ** -0.5
SB_BIAS_LO = -9.0
SB_BIAS_HI = -5.0
Q_BLOCK = 128
POOL_WINDOWS = (2, 4, 8, 16)
POOL_GROUPS = len(POOL_WINDOWS)
POOL_WIDTH = D_MODEL
POOL_GROUP_DIM = POOL_WIDTH // POOL_GROUPS
POOL_HIST = max(POOL_WINDOWS) - 1
RMS_EPS = 1e-6

kernel_name = 'hybrid_stickbreak_pool_decoder_step'


def _rmsnorm(x, g):
    xf = x.astype(jnp.float32)
    y = xf * lax.rsqrt(jnp.mean(xf * xf, axis=-1, keepdims=True) + RMS_EPS)
    return (y * g.astype(jnp.float32)).astype(x.dtype)


def _modulate(x, c, g, w_mod, b_mod):
    m = (jax.nn.silu(c) @ w_mod + b_mod)[:, None, :]
    shift, scale, gate = jnp.split(m, 3, axis=-1)
    return _rmsnorm(x, g) * (1 + scale) + shift, gate


def _sb_weights(z, causal):
    log_keep = jnp.where(causal, jax.nn.log_sigmoid(-z), 0.0)
    shifted = jnp.concatenate([log_keep[..., 1:], jnp.zeros_like(log_keep[..., :1])], axis=-1)
    suffix = lax.cumsum(shifted, axis=z.ndim - 1, reverse=True)
    return jnp.where(causal, jnp.exp(jax.nn.log_sigmoid(z) + suffix), 0.0)


def _sb_project(h, w_in, gq, gk):
    b, t, _ = h.shape
    q, k, v, gate = jnp.split(h @ w_in, 4, axis=-1)
    shp = (b, t, SB_HEADS, SB_HEAD_DIM)
    return _rmsnorm(q.reshape(shp), gq), _rmsnorm(k.reshape(shp), gk), v.reshape(shp), gate


def _sb_out(o, gate, w_out):
    b, t = o.shape[:2]
    return (o.reshape(b, t, SB_WIDTH) * jax.nn.silu(gate)) @ w_out


def _sb_prompt(h, w_in, w_out, gq, gk, bias):
    b, t, _ = h.shape
    q, k, v, gate = _sb_project(h, w_in, gq, gk)
    qb = min(Q_BLOCK, t)
    nb = t // qb
    k_pos = jnp.arange(t)
    q_blocks = q.reshape(b, nb, qb, SB_HEADS, SB_HEAD_DIM).transpose(1, 0, 2, 3, 4)
    q_pos = jnp.arange(t).reshape(nb, qb)
    bias_f = bias.astype(jnp.float32)[None, :, None, None]

    def block(args):
        q_blk, pos = args
        z = jnp.einsum('bqhd,bkhd->bhqk', q_blk, k, preferred_element_type=jnp.float32) * SB_SCALE + bias_f
        a = _sb_weights(z, k_pos[None, :] < pos[:, None])
        return jnp.einsum('bhqk,bkhd->bqhd', a.astype(v.dtype), v)

    o = lax.map(block, (q_blocks, q_pos))
    o = o.transpose(1, 0, 2, 3, 4).reshape(b, t, SB_HEADS, SB_HEAD_DIM)
    return _sb_out(o, gate, w_out), k, v


def _sb_sample(h, k_pages, v_pages, w_in, w_out, gq, gk, bias):
    b, t, _ = h.shape
    q, k, v, gate = _sb_project(h, w_in, gq, gk)
    past = k_pages.shape[1] * PAGE_SIZE
    k_past = k_pages.reshape(b, past, SB_HEADS, SB_HEAD_DIM)
    v_past = v_pages.reshape(b, past, SB_HEADS, SB_HEAD_DIM)
    z = jnp.concatenate([
        jnp.einsum('bqhd,bkhd->bhqk', q, k_past, preferred_element_type=jnp.float32),
        jnp.einsum('bqhd,bkhd->bhqk', q, k, preferred_element_type=jnp.float32)], axis=-1) * SB_SCALE
    z = z + bias.astype(jnp.float32)[None, :, None, None]
    q_pos = past + jnp.arange(t)
    k_pos = jnp.arange(past + t)
    a = _sb_weights(z, k_pos[None, :] < q_pos[:, None]).astype(v.dtype)
    o = (jnp.einsum('bhqk,bkhd->bqhd', a[..., :past], v_past)
         + jnp.einsum('bhqk,bkhd->bqhd', a[..., past:], v))
    return _sb_out(o, gate, w_out), k, v


def _pool_mix(h, hist, pos0, w_in, w_grp, scale, w_out):
    b, t, _ = h.shape
    u, gate = jnp.split(h @ w_in, 2, axis=-1)
    ext = jnp.concatenate([hist.astype(u.dtype), u], axis=1)
    cs = jnp.cumsum(ext.astype(jnp.float32), axis=1)
    cs = jnp.concatenate([jnp.zeros_like(cs[:, :1]), cs], axis=1)
    end = cs[:, POOL_HIST + 1:]
    pos = pos0 + jnp.arange(t)
    means = []
    for g, w in enumerate(POOL_WINDOWS):
        sl = slice(g * POOL_GROUP_DIM, (g + 1) * POOL_GROUP_DIM)
        start = cs[:, POOL_HIST + 1 - w: POOL_HIST + 1 - w + t, sl]
        cnt = jnp.minimum(w, pos + 1).astype(jnp.float32)[None, :, None]
        means.append((end[..., sl] - start) / cnt)
    pooled = jnp.stack(means, axis=2)
    diff = pooled - u.reshape(b, t, POOL_GROUPS, POOL_GROUP_DIM).astype(jnp.float32)
    mixed = jnp.einsum('btgc,gcd->btgd', diff.astype(u.dtype), w_grp) * scale.reshape(POOL_GROUPS, POOL_GROUP_DIM)
    y = (mixed.reshape(b, t, POOL_WIDTH) * jax.nn.silu(gate)) @ w_out
    return y, ext[:, -POOL_HIST:]


def setup_inputs(seed: int = 0) -> dict:
    key = jax.random.key(seed)
    ks = jax.random.split(key, 21)
    f32 = jnp.float32
    n_pages = PAST_LEN // PAGE_SIZE
    n_used = DEC_BATCH * n_pages
    n_phys = n_used + n_used // 4
    perm = jax.random.permutation(ks[0], n_phys).astype(jnp.int32)
    page_table = perm[:n_used].reshape(DEC_BATCH, n_pages)
    kv_shape = (N_A_LAYERS, n_phys, PAGE_SIZE, SB_HEADS, SB_HEAD_DIM)
    head_bias = jnp.linspace(SB_BIAS_LO, SB_BIAS_HI, SB_HEADS, dtype=f32)[None, :]
    return {
        'x_prompt': jax.random.normal(ks[1], (BATCH, SEQ, D_MODEL), f32),
        'x_sample': jax.random.normal(ks[2], (DEC_BATCH, DEC_SEQ, D_MODEL), f32),
        'cache_k': jax.random.normal(ks[3], kv_shape, f32),
        'cache_v': jax.random.normal(ks[4], kv_shape, f32),
        'state_pool': jax.random.normal(ks[5], (N_B_LAYERS, DEC_BATCH, POOL_HIST, POOL_WIDTH), f32),
        'page_table': page_table,
        'c_prompt': jax.random.normal(ks[6], (BATCH, D_MODEL), f32),
        'c_sample': jax.random.normal(ks[7], (DEC_BATCH, D_MODEL), f32),
        'norm_g': 1.0 + 0.02 * jax.random.normal(ks[8], (DEPTH, D_MODEL), f32),
        'w_mod': 0.5 * D_MODEL ** -0.5 * jax.random.normal(ks[9], (DEPTH, D_MODEL, 3 * D_MODEL), f32),
        'b_mod': 0.02 * jax.random.normal(ks[10], (DEPTH, 3 * D_MODEL), f32),
        'w_in_a': D_MODEL ** -0.5 * jax.random.normal(ks[11], (N_A_LAYERS, D_MODEL, 4 * SB_WIDTH), f32),
        'q_norm_g': 1.0 + 0.02 * jax.random.normal(ks[12], (N_A_LAYERS, SB_HEAD_DIM), f32),
        'k_norm_g': 1.0 + 0.02 * jax.random.normal(ks[13], (N_A_LAYERS, SB_HEAD_DIM), f32),
        'sb_bias': head_bias + 0.1 * jax.random.normal(ks[19], (N_A_LAYERS, SB_HEADS), f32),
        'w_out_a': SB_WIDTH ** -0.5 * jax.random.normal(ks[14], (N_A_LAYERS, SB_WIDTH, D_MODEL), f32),
        'w_in_b': D_MODEL ** -0.5 * jax.random.normal(ks[15], (N_B_LAYERS, D_MODEL, 2 * POOL_WIDTH), f32),
        'w_grp_b': POOL_GROUP_DIM ** -0.5 * jax.random.normal(ks[16], (N_B_LAYERS, POOL_GROUPS, POOL_GROUP_DIM, POOL_GROUP_DIM), f32),
        'scale_b': 1.0 + 0.02 * jax.random.normal(ks[17], (N_B_LAYERS, POOL_WIDTH), f32),
        'w_out_b': POOL_WIDTH ** -0.5 * jax.random.normal(ks[18], (N_B_LAYERS, POOL_WIDTH, D_MODEL), f32),
    }


def reference(x_prompt, x_sample, cache_k, cache_v, state_pool, page_table, c_prompt, c_sample,
              norm_g, w_mod, b_mod, w_in_a, q_norm_g, k_norm_g, sb_bias, w_out_a,
              w_in_b, w_grp_b, scale_b, w_out_b):
    past_len = page_table.shape[1] * PAGE_SIZE
    xp, xs = x_prompt, x_sample
    nk_p, nv_p, nk_s, nv_s, npool_p, npool_s = [], [], [], [], [], []
    for i in range(DEPTH):
        hp, gp = _modulate(xp, c_prompt, norm_g[i], w_mod[i], b_mod[i])
        hs, gs = _modulate(xs, c_sample, norm_g[i], w_mod[i], b_mod[i])
        j = i // N_MIXERS
        if i % N_MIXERS == 0:
            yp, kp, vp = _sb_prompt(hp, w_in_a[j], w_out_a[j], q_norm_g[j], k_norm_g[j], sb_bias[j])
            ys, ks_, vs_ = _sb_sample(hs, cache_k[j][page_table], cache_v[j][page_table],
                                      w_in_a[j], w_out_a[j], q_norm_g[j], k_norm_g[j], sb_bias[j])
            nk_p.append(kp); nv_p.append(vp); nk_s.append(ks_); nv_s.append(vs_)
        else:
            hist0 = jnp.zeros((xp.shape[0], POOL_HIST, POOL_WIDTH), xp.dtype)
            yp, pp = _pool_mix(hp, hist0, 0, w_in_b[j], w_grp_b[j], scale_b[j], w_out_b[j])
            ys, ps = _pool_mix(hs, state_pool[j], past_len, w_in_b[j], w_grp_b[j], scale_b[j], w_out_b[j])
            npool_p.append(pp); npool_s.append(ps)
        xp = xp + gp * yp
        xs = xs + gs * ys
    return (xp, xs, jnp.stack(nk_p), jnp.stack(nv_p), jnp.stack(nk_s), jnp.stack(nv_s),
            jnp.stack(npool_p), jnp.stack(npool_s))
```

```python
import functools

import jax
import jax.numpy as jnp
from jax import lax
from jax.experimental import pallas as pl
from jax.experimental.pallas import tpu as pltpu

F32 = jnp.float32
BF16 = jnp.bfloat16

RMS_EPS = 1e-6
HEAD_DIM = 128
PAGE = 128
POOL_WINDOWS = (2, 4, 8, 16)
POOL_HIST = max(POOL_WINDOWS) - 1

VMEM_LIMIT = 52 * 1024 * 1024

Q_TILE = 256
K_TILE = 256
PAGES_PER_STEP = 4
POOL_SUB = 128


def _silu(x):
    return x * (1.0 / (1.0 + jnp.exp(-x)))


def _softplus(z):
    return jnp.maximum(z, 0.0) + jnp.log(1.0 + jnp.exp(-jnp.abs(z)))


def _split_bf16(x):
    hi = x.astype(BF16)
    lo = (x - hi.astype(F32)).astype(BF16)
    return hi, lo


def _params(*sem):
    return pltpu.CompilerParams(dimension_semantics=sem, vmem_limit_bytes=VMEM_LIMIT)


def _mod_kernel(c_ref, w_ref, b_ref, o_ref):
    s = _silu(c_ref[...]).astype(BF16)
    w = w_ref[...].astype(BF16)
    o_ref[...] = jnp.dot(s, w, preferred_element_type=F32) + b_ref[...]


def _modulation(c_all, w_mod, b_mod):
    depth, d, n = w_mod.shape
    rows = c_all.shape[0]
    tn = 1024
    return pl.pallas_call(
        _mod_kernel,
        grid=(depth, n // tn),
        in_specs=[
            pl.BlockSpec((rows, d), lambda l, j: (0, 0)),
            pl.BlockSpec((None, d, tn), lambda l, j: (l, 0, j)),
            pl.BlockSpec((None, 1, tn), lambda l, j: (l, 0, j)),
        ],
        out_specs=pl.BlockSpec((None, rows, tn), lambda l, j: (l, 0, j)),
        out_shape=jax.ShapeDtypeStruct((depth, rows, n), F32),
        compiler_params=_params("parallel", "parallel"),
        name="modulation",
    )(c_all, w_mod, b_mod.reshape(depth, 1, n))


def _modulated_norm(x_ref, shift_ref, scale_ref, g_ref, h_ref):
    x = x_ref[...]
    y = x * lax.rsqrt(jnp.mean(x * x, axis=-1, keepdims=True) + RMS_EPS)
    h = (y * g_ref[...]) * (1.0 + scale_ref[...]) + shift_ref[...]
    h_ref[...] = h.astype(BF16)


def _head_rmsnorm(a, g):
    return a * lax.rsqrt(jnp.mean(a * a, axis=-1, keepdims=True) + RMS_EPS) * g


def _proj_a_kernel(x_ref, shift_ref, scale_ref, g_ref, wq_ref, wk_ref, wv_ref, wg_ref,
                   gq_ref, gk_ref,
                   q_ref, kf_ref, kb_ref, vf_ref, vb_ref, gate_ref, h_ref, *, q_scale):
    @pl.when(pl.program_id(1) == 0)
    def _():
        _modulated_norm(x_ref, shift_ref, scale_ref, g_ref, h_ref)

    h = h_ref[...]
    n_heads = q_ref.shape[1] // HEAD_DIM
    q = jnp.dot(h, wq_ref[...], preferred_element_type=F32)
    k = jnp.dot(h, wk_ref[...], preferred_element_type=F32)
    for hd in range(n_heads):
        sl = slice(hd * HEAD_DIM, (hd + 1) * HEAD_DIM)
        qn = _head_rmsnorm(q[:, sl], gq_ref[...])
        q_ref[:, sl] = (qn * q_scale).astype(q_ref.dtype)
        kn = _head_rmsnorm(k[:, sl], gk_ref[...])
        kf_ref[:, sl] = kn
        kb_ref[:, sl] = kn.astype(kb_ref.dtype)
    v = jnp.dot(h, wv_ref[...], preferred_element_type=F32)
    vf_ref[...] = v
    vb_ref[...] = v.astype(vb_ref.dtype)
    gate_ref[...] = jnp.dot(h, wg_ref[...], preferred_element_type=F32)


def _mod_specs(mod, tm, d, rows_per_seq):
    if rows_per_seq is None:
        return [pl.BlockSpec((tm, d), functools.partial(lambda i, j, c: (i, c), c=c))
                for c in range(3)]
    tiles = rows_per_seq // tm
    return [pl.BlockSpec((None, 1, d),
                         functools.partial(lambda i, j, c: (i // tiles, 0, c), c=c))
            for c in range(3)]


def _proj_a(x, mod, rows_per_seq, norm_g, w4, gq, gk, small_dtype, tm, tn=512):
    rows, d = x.shape
    width = w4[0].shape[1]
    shift_spec, scale_spec, _ = _mod_specs(mod, tm, d, rows_per_seq)
    w_spec = pl.BlockSpec((d, tn), lambda i, j: (0, j))
    o_spec = pl.BlockSpec((tm, tn), lambda i, j: (i, j))
    row_spec = pl.BlockSpec((1, HEAD_DIM), lambda i, j: (0, 0))
    f32_out = jax.ShapeDtypeStruct((rows, width), F32)
    small_out = jax.ShapeDtypeStruct((rows, width), small_dtype)
    return pl.pallas_call(
        functools.partial(_proj_a_kernel, q_scale=HEAD_DIM ** -0.5),
        grid=(rows // tm, width // tn),
        in_specs=[pl.BlockSpec((tm, d), lambda i, j: (i, 0)), shift_spec, scale_spec,
                  pl.BlockSpec((1, d), lambda i, j: (0, 0)),
                  w_spec, w_spec, w_spec, w_spec, row_spec, row_spec],
        out_specs=[o_spec] * 6,
        out_shape=[small_out, f32_out, small_out, f32_out, small_out, f32_out],
        scratch_shapes=[pltpu.VMEM((tm, d), BF16)],
        compiler_params=_params("parallel", "arbitrary"),
        name="proj_a",
    )(x, mod, mod, norm_g, *w4, gq, gk)


def _proj_b_kernel(x_ref, shift_ref, scale_ref, g_ref, wu_ref, wg_ref, u_ref, gate_ref, h_ref):
    @pl.when(pl.program_id(1) == 0)
    def _():
        _modulated_norm(x_ref, shift_ref, scale_ref, g_ref, h_ref)

    h = h_ref[...]
    u_ref[...] = jnp.dot(h, wu_ref[...], preferred_element_type=F32)
    gate_ref[...] = jnp.dot(h, wg_ref[...], preferred_element_type=F32)


def _proj_b(x, mod, rows_per_seq, norm_g, w2, tm, tn=512):
    rows, d = x.shape
    width = w2[0].shape[1]
    shift_spec, scale_spec, _ = _mod_specs(mod, tm, d, rows_per_seq)
    w_spec = pl.BlockSpec((d, tn), lambda i, j: (0, j))
    o_spec = pl.BlockSpec((tm, tn), lambda i, j: (i, j))
    f32_out = jax.ShapeDtypeStruct((rows, width), F32)
    return pl.pallas_call(
        _proj_b_kernel,
        grid=(rows // tm, width // tn),
        in_specs=[pl.BlockSpec((tm, d), lambda i, j: (i, 0)), shift_spec, scale_spec,
                  pl.BlockSpec((1, d), lambda i, j: (0, 0)), w_spec, w_spec],
        out_specs=[o_spec, o_spec],
        out_shape=[f32_out, f32_out],
        scratch_shapes=[pltpu.VMEM((tm, d), BF16)],
        compiler_params=_params("parallel", "arbitrary"),
        name="proj_b",
    )(x, mod, mod, norm_g, *w2)


def _out_kernel(inp_ref, w_ref, x_ref, gmod_ref, o_ref):
    y = jnp.dot(inp_ref[...].astype(BF16), w_ref[...], preferred_element_type=F32)
    o_ref[...] = x_ref[...] + gmod_ref[...] * y


def _out_proj(inp, w_out, x, mod, rows_per_seq, tm, tn=1024):
    rows, d = x.shape
    kdim = inp.shape[1]
    if rows_per_seq is None:
        gmod_spec = pl.BlockSpec((tm, tn), lambda i, j: (i, 2 * (d // tn) + j))
    else:
        tiles = rows_per_seq // tm
        gmod_spec = pl.BlockSpec((None, 1, tn), lambda i, j: (i // tiles, 0, 2 * (d // tn) + j))
    return pl.pallas_call(
        _out_kernel,
        grid=(rows // tm, d // tn),
        in_specs=[pl.BlockSpec((tm, kdim), lambda i, j: (i, 0)),
                  pl.BlockSpec((kdim, tn), lambda i, j: (0, j)),
                  pl.BlockSpec((tm, tn), lambda i, j: (i, j)),
                  gmod_spec],
        out_specs=pl.BlockSpec((tm, tn), lambda i, j: (i, j)),
        out_shape=jax.ShapeDtypeStruct((rows, d), F32),
        compiler_params=_params("parallel", "parallel"),
        name="out_proj",
    )(inp, w_out, x, mod)


def _sb_block(z, tri, mask):
    sp = _softplus(z)
    l = sp if mask is None else jnp.where(mask, sp, 0.0)
    hi, lo = _split_bf16(l)
    later = (jnp.dot(hi, tri, preferred_element_type=F32)
             + jnp.dot(lo, tri, preferred_element_type=F32))
    p = jnp.exp(z - sp - later)
    if mask is not None:
        p = jnp.where(mask, p, 0.0)
    return p, jnp.sum(l, axis=1, keepdims=True)


def _attn_prompt_kernel(bias_ref, q_ref, k_ref, v_ref, gate_ref, tri_ref, o_ref,
                        acc_ref, carry_ref):
    hd = pl.program_id(1)
    qi = pl.program_id(2)
    q = q_ref[...]
    bias = bias_ref[hd]
    tri = tri_ref[...]

    def scores(kj):
        ks = pl.multiple_of(kj * K_TILE, K_TILE)
        k = k_ref[pl.ds(ks, K_TILE), :]
        v = v_ref[pl.ds(ks, K_TILE), :]
        z = lax.dot_general(q, k, (((1,), (1,)), ((), ())), preferred_element_type=F32)
        return z + bias, v

    z, v = scores(qi)
    row = lax.broadcasted_iota(jnp.int32, z.shape, 0)
    col = lax.broadcasted_iota(jnp.int32, z.shape, 1)
    p, tot = _sb_block(z, tri, col < row)
    acc_ref[...] = jnp.dot(p.astype(BF16), v, preferred_element_type=F32)
    carry_ref[...] = jnp.broadcast_to(tot, carry_ref.shape)

    def body(t, _):
        z, v = scores(qi - 1 - t)
        p, tot = _sb_block(z, tri, None)
        carry = carry_ref[...]
        pv = jnp.dot(p.astype(BF16), v, preferred_element_type=F32)
        acc_ref[...] += jnp.exp(-carry) * pv
        carry_ref[...] = carry + tot
        return 0

    lax.fori_loop(0, qi, body, 0)
    o_ref[...] = (acc_ref[...] * _silu(gate_ref[...])).astype(o_ref.dtype)


def _attn_prompt(q, k, v, gate, bias, tri, batch, seq):
    rows, width = q.shape
    n_heads = width // HEAD_DIM
    nq = seq // Q_TILE
    q_spec = pl.BlockSpec((Q_TILE, HEAD_DIM), lambda b, h, i: (b * nq + i, h))
    kv_spec = pl.BlockSpec((seq, HEAD_DIM), lambda b, h, i: (b, h))
    return pl.pallas_call(
        _attn_prompt_kernel,
        grid=(batch, n_heads, nq),
        in_specs=[pl.BlockSpec(memory_space=pltpu.SMEM), q_spec, kv_spec, kv_spec, q_spec,
                  pl.BlockSpec((K_TILE, K_TILE), lambda b, h, i: (0, 0))],
        out_specs=q_spec,
        out_shape=jax.ShapeDtypeStruct((rows, width), BF16),
        scratch_shapes=[pltpu.VMEM((Q_TILE, HEAD_DIM), F32), pltpu.VMEM((Q_TILE, HEAD_DIM), F32)],
        compiler_params=_params("parallel", "parallel", "arbitrary"),
        name="attn_prompt",
    )(bias, q, k, v, gate, tri)


def _attn_sample_kernel(pt_ref, q_ref, kn_ref, vn_ref, gate_ref, bias_ref, tri_ref, *rest,
                        n_heads):
    del pt_ref
    npg = PAGES_PER_STEP
    k_pages, v_pages = rest[:npg], rest[npg:2 * npg]
    o_ref, qbd_ref, knew_ref, vnew_ref, acc_ref, carry_ref = rest[2 * npg:]
    g = pl.program_id(1)
    n_tok = q_ref.shape[0]
    n_chunks = n_heads // 2
    cw = 2 * HEAD_DIM
    rows = n_heads * n_tok

    def new_chunk(ref, c):
        return ref[:, c * cw:(c + 1) * cw].astype(BF16)

    def page_chunk(parts, c):
        def head(r, hd):
            return r[pl.ds(hd, PAGE, stride=n_heads), :]
        return jnp.concatenate(
            [jnp.concatenate([head(r, 2 * c), head(r, 2 * c + 1)], axis=1) for r in parts],
            axis=0).astype(BF16)

    def attend(load_k, load_v, width, mask):
        z = [lax.dot_general(qbd_ref[c], load_k(c), (((1,), (1,)), ((), ())),
                             preferred_element_type=F32) for c in range(n_chunks)]
        z = jnp.concatenate(z, axis=0) + bias_ref[...]
        p, tot = _sb_block(z, tri_ref[:width, :width], mask)
        carry = carry_ref[...]
        w_old = jnp.exp(-carry)
        p = p.astype(BF16)
        for c in range(n_chunks):
            rs = slice(c * 2 * n_tok, (c + 1) * 2 * n_tok)
            pv = jnp.dot(p[rs], load_v(c), preferred_element_type=F32)
            acc_ref[rs, :] += w_old[rs, :1] * pv
        carry_ref[...] = carry + tot

    @pl.when(g == 0)
    def _():
        acc_ref[...] = jnp.zeros_like(acc_ref)
        carry_ref[...] = jnp.zeros_like(carry_ref)
        zero = jnp.zeros((n_tok, HEAD_DIM), F32)
        for c in range(n_chunks):
            q0 = q_ref[:, (2 * c) * HEAD_DIM:(2 * c + 1) * HEAD_DIM]
            q1 = q_ref[:, (2 * c + 1) * HEAD_DIM:(2 * c + 2) * HEAD_DIM]
            top = jnp.concatenate([q0, zero], axis=1)
            bot = jnp.concatenate([zero, q1], axis=1)
            qbd_ref[c] = jnp.concatenate([top, bot], axis=0).astype(BF16)
        knew_ref[...] = jnp.zeros_like(knew_ref)
        vnew_ref[...] = jnp.zeros_like(vnew_ref)
        knew_ref[:n_tok, :] = kn_ref[...]
        vnew_ref[:n_tok, :] = vn_ref[...]
        tok = lax.broadcasted_iota(jnp.int32, (rows, PAGE), 0) % n_tok
        col = lax.broadcasted_iota(jnp.int32, (rows, PAGE), 1)
        attend(functools.partial(new_chunk, knew_ref), functools.partial(new_chunk, vnew_ref),
               PAGE, col < tok)

    attend(functools.partial(page_chunk, k_pages), functools.partial(page_chunk, v_pages),
           npg * PAGE, None)

    @pl.when(g == pl.num_programs(1) - 1)
    def _():
        for hd in range(n_heads):
            c, hh = hd // 2, hd % 2
            r0 = (c * 2 + hh) * n_tok
            o = acc_ref[r0:r0 + n_tok, hh * HEAD_DIM:(hh + 1) * HEAD_DIM]
            sl = slice(hd * HEAD_DIM, (hd + 1) * HEAD_DIM)
            o_ref[:, sl] = o * _silu(gate_ref[:, sl])


def _attn_sample(page_table, q, k_new, v_new, gate, bias_rows, tri, cache_k, cache_v, layer):
    batch, n_pages = page_table.shape
    rows, width = q.shape
    n_tok = rows // batch
    n_heads = width // HEAD_DIM
    npg = PAGES_PER_STEP
    n_groups = n_pages // npg
    tok_spec = pl.BlockSpec((n_tok, width), lambda b, g, pt: (b, 0))

    def page_spec(p):
        def index(b, g, pt):
            return (layer, pt[b, (n_groups - 1 - g) * npg + p], 0, 0)
        return pl.BlockSpec((None, None, PAGE * n_heads, HEAD_DIM), index)

    page_specs = [page_spec(p) for p in range(npg)]
    grid_spec = pltpu.PrefetchScalarGridSpec(
        num_scalar_prefetch=1,
        grid=(batch, n_groups),
        in_specs=[tok_spec, tok_spec, tok_spec, tok_spec,
                  pl.BlockSpec((n_heads * n_tok, 1), lambda b, g, pt: (0, 0)),
                  pl.BlockSpec((npg * PAGE, npg * PAGE), lambda b, g, pt: (0, 0))]
                 + page_specs + page_specs,
        out_specs=tok_spec,
        scratch_shapes=[pltpu.VMEM((n_heads // 2, 2 * n_tok, 2 * HEAD_DIM), BF16),
                        pltpu.VMEM((PAGE, width), F32),
                        pltpu.VMEM((PAGE, width), F32),
                        pltpu.VMEM((n_heads * n_tok, 2 * HEAD_DIM), F32),
                        pltpu.VMEM((n_heads * n_tok, HEAD_DIM), F32)],
    )
    return pl.pallas_call(
        functools.partial(_attn_sample_kernel, n_heads=n_heads),
        grid_spec=grid_spec,
        out_shape=jax.ShapeDtypeStruct((rows, width), F32),
        compiler_params=_params("parallel", "arbitrary"),
        name="attn_sample",
    )(page_table, q, k_new, v_new, gate, bias_rows, tri,
      *([cache_k] * npg), *([cache_v] * npg))


def _group_mix(mean, u, gate, w_grp, scale):
    diff = (mean - u).astype(BF16)
    return jnp.dot(diff, w_grp, preferred_element_type=F32) * scale * _silu(gate)


def _pool_prompt_kernel(u_ref, halo_ref, gate_ref, wgrp_ref, scale_ref, band_ref, o_ref,
                        hi_ref, lo_ref, *, tiles_per_seq):
    tm = u_ref.shape[0]
    gdim = wgrp_ref.shape[1]
    tile = pl.program_id(0) % tiles_per_seq
    halo = jnp.where(tile == 0, 0.0, halo_ref[...])
    hi, lo = _split_bf16(halo)
    hi_ref[:POOL_SUB, :] = hi
    lo_ref[:POOL_SUB, :] = lo
    hi, lo = _split_bf16(u_ref[...])
    hi_ref[POOL_SUB:, :] = hi
    lo_ref[POOL_SUB:, :] = lo
    for m in range(tm // POOL_SUB):
        r0 = m * POOL_SUB
        pos = tile * tm + r0 + lax.broadcasted_iota(jnp.int32, (POOL_SUB, 1), 0)
        for g, w in enumerate(POOL_WINDOWS):
            cs = slice(g * gdim, (g + 1) * gdim)
            band = band_ref[g]
            total = (jnp.dot(band, hi_ref[r0:r0 + 2 * POOL_SUB, cs], preferred_element_type=F32)
                     + jnp.dot(band, lo_ref[r0:r0 + 2 * POOL_SUB, cs], preferred_element_type=F32))
            inv_cnt = 1.0 / jnp.minimum(w, pos + 1).astype(F32)
            mixed = _group_mix(total * inv_cnt, u_ref[r0:r0 + POOL_SUB, cs],
                               gate_ref[r0:r0 + POOL_SUB, cs], wgrp_ref[g], scale_ref[:, cs])
            o_ref[r0:r0 + POOL_SUB, cs] = mixed.astype(o_ref.dtype)


def _pool_prompt(u, gate, w_grp, scale, bands, seq, tm=512):
    rows, width = u.shape
    sub_per_tile = tm // POOL_SUB
    tile_spec = pl.BlockSpec((tm, width), lambda i: (i, 0))
    return pl.pallas_call(
        functools.partial(_pool_prompt_kernel, tiles_per_seq=seq // tm),
        grid=(rows // tm,),
        in_specs=[tile_spec,
                  pl.BlockSpec((POOL_SUB, width), lambda i: (jnp.maximum(i * sub_per_tile - 1, 0), 0)),
                  tile_spec,
                  pl.BlockSpec(w_grp.shape, lambda i: (0, 0, 0)),
                  pl.BlockSpec((1, width), lambda i: (0, 0)),
                  pl.BlockSpec(bands.shape, lambda i: (0, 0, 0))],
        out_specs=tile_spec,
        out_shape=jax.ShapeDtypeStruct((rows, width), BF16),
        scratch_shapes=[pltpu.VMEM((tm + POOL_SUB, width), BF16),
                        pltpu.VMEM((tm + POOL_SUB, width), BF16)],
        compiler_params=_params("parallel"),
        name="pool_prompt",
    )(u, u, gate, w_grp, scale, bands)


def _pool_sample_kernel(ext_ref, gate_ref, wgrp_ref, scale_ref, o_ref):
    n_tok = gate_ref.shape[0]
    gdim = wgrp_ref.shape[1]
    last = ext_ref.shape[0] - n_tok
    for g, w in enumerate(POOL_WINDOWS):
        cs = slice(g * gdim, (g + 1) * gdim)
        u = ext_ref[last:last + n_tok, cs]
        total = u
        for back in range(1, w):
            total = total + ext_ref[last - back:last - back + n_tok, cs]
        o_ref[:, cs] = _group_mix(total / float(w), u, gate_ref[:, cs], wgrp_ref[g], scale_ref[:, cs])


def _pool_sample(ext, gate, w_grp, scale):
    batch, ext_rows, width = ext.shape
    rows = gate.shape[0]
    n_tok = rows // batch
    tok_spec = pl.BlockSpec((n_tok, width), lambda b: (b, 0))
    return pl.pallas_call(
        _pool_sample_kernel,
        grid=(batch,),
        in_specs=[pl.BlockSpec((None, ext_rows, width), lambda b: (b, 0, 0)),
                  tok_spec,
                  pl.BlockSpec(w_grp.shape, lambda b: (0, 0, 0)),
                  pl.BlockSpec((1, width), lambda b: (0, 0))],
        out_specs=tok_spec,
        out_shape=jax.ShapeDtypeStruct((rows, width), F32),
        compiler_params=_params("parallel"),
        name="pool_sample",
    )(ext, gate, w_grp, scale)


def _strict_lower(n):
    j = lax.broadcasted_iota(jnp.int32, (n, n), 0)
    s = lax.broadcasted_iota(jnp.int32, (n, n), 1)
    return (j > s).astype(BF16)


def _pool_bands():
    r = lax.broadcasted_iota(jnp.int32, (POOL_SUB, 2 * POOL_SUB), 0)
    c = lax.broadcasted_iota(jnp.int32, (POOL_SUB, 2 * POOL_SUB), 1)
    back = POOL_SUB + r - c
    return jnp.stack([((back >= 0) & (back < w)).astype(BF16) for w in POOL_WINDOWS])


def kernel(x_prompt, x_sample, cache_k, cache_v, state_pool, page_table, c_prompt, c_sample,
           norm_g, w_mod, b_mod, w_in_a, q_norm_g, k_norm_g, sb_bias, w_out_a,
           w_in_b, w_grp_b, scale_b, w_out_b):
    batch, seq, d = x_prompt.shape
    dec_batch, dec_seq, _ = x_sample.shape
    depth = norm_g.shape[0]
    n_heads = sb_bias.shape[1]
    width_a = n_heads * HEAD_DIM
    n_phys = cache_k.shape[1]
    rows_p, rows_s = batch * seq, dec_batch * dec_seq
    assert cache_k.shape[2] == PAGE and cache_k.shape[4] == HEAD_DIM
    assert state_pool.shape[2] == POOL_HIST and seq % 512 == 0

    n_seq = batch + dec_batch
    pad = (-n_seq) % 8
    c_all = jnp.concatenate([c_prompt, c_sample, jnp.zeros((pad, d), F32)], axis=0)
    mods = _modulation(c_all, w_mod, b_mod)

    cache_k2 = cache_k.reshape(cache_k.shape[0], n_phys, PAGE * n_heads, HEAD_DIM)
    cache_v2 = cache_v.reshape(cache_v.shape[0], n_phys, PAGE * n_heads, HEAD_DIM)
    tri_p = _strict_lower(K_TILE)
    tri_s = _strict_lower(PAGES_PER_STEP * PAGE)
    bands = _pool_bands()

    xp = x_prompt.reshape(rows_p, d)
    xs = x_sample.reshape(rows_s, d)
    nk_p, nv_p, nk_s, nv_s, npool_p, npool_s = [], [], [], [], [], []
    for i in range(depth):
        mod_p = mods[i, :batch].reshape(batch, 1, 3 * d)
        mod_s = jnp.repeat(mods[i, batch:n_seq], dec_seq, axis=0)
        g_i = norm_g[i].reshape(1, d)
        j = i // 2
        if i % 2 == 0:
            w4 = [w_in_a[j, :, s * width_a:(s + 1) * width_a].astype(BF16) for s in range(4)]
            w_out = w_out_a[j].astype(BF16)
            gq = q_norm_g[j].reshape(1, HEAD_DIM)
            gk = k_norm_g[j].reshape(1, HEAD_DIM)
            q_p, kf_p, kb_p, vf_p, vb_p, gate_p = _proj_a(
                xp, mod_p, seq, g_i, w4, gq, gk, BF16, tm=512)
            q_s, kf_s, _, vf_s, _, gate_s = _proj_a(
                xs, mod_s, None, g_i, w4, gq, gk, F32, tm=rows_s)
            o_p = _attn_prompt(q_p, kb_p, vb_p, gate_p, sb_bias[j], tri_p, batch, seq)
            bias_rows = jnp.repeat(sb_bias[j], dec_seq).reshape(n_heads * dec_seq, 1)
            o_s = _attn_sample(page_table, q_s, kf_s, vf_s, gate_s, bias_rows, tri_s,
                               cache_k2, cache_v2, j)
            nk_p.append(kf_p.reshape(batch, seq, n_heads, HEAD_DIM))
            nv_p.append(vf_p.reshape(batch, seq, n_heads, HEAD_DIM))
            nk_s.append(kf_s.reshape(dec_batch, dec_seq, n_heads, HEAD_DIM))
            nv_s.append(vf_s.reshape(dec_batch, dec_seq, n_heads, HEAD_DIM))
        else:
            width_b = w_in_b.shape[2] // 2
            w2 = [w_in_b[j, :, s * width_b:(s + 1) * width_b].astype(BF16) for s in range(2)]
            w_out = w_out_b[j].astype(BF16)
            w_grp = w_grp_b[j].astype(BF16)
            scale = scale_b[j].reshape(1, width_b)
            u_p, gate_p = _proj_b(xp, mod_p, seq, g_i, w2, tm=512)
            u_s, gate_s = _proj_b(xs, mod_s, None, g_i, w2, tm=rows_s)
            o_p = _pool_prompt(u_p, gate_p, w_grp, scale, bands, seq)
            ext = jnp.concatenate([state_pool[j], u_s.reshape(dec_batch, dec_seq, width_b)], axis=1)
            front = (-POOL_HIST) % 8
            ext_pad = jnp.pad(ext, ((0, 0), (front, 0), (0, 0)))
            o_s = _pool_sample(ext_pad, gate_s, w_grp, scale)
            npool_p.append(u_p.reshape(batch, seq, width_b)[:, seq - POOL_HIST:])
            npool_s.append(ext[:, dec_seq:])
        xp = _out_proj(o_p, w_out, xp, mod_p, seq, tm=512)
        xs = _out_proj(o_s, w_out, xs, mod_s, None, tm=rows_s)
    return (xp.reshape(batch, seq, d), xs.reshape(dec_batch, dec_seq, d),
            jnp.stack(nk_p), jnp.stack(nv_p), jnp.stack(nk_s), jnp.stack(nv_s),
            jnp.stack(npool_p), jnp.stack(npool_s))
```

```python
import functools

import jax
import jax.numpy as jnp
from jax import lax
from jax.experimental import pallas as pl
from jax.experimental.pallas import tpu as pltpu

F32 = jnp.float32
BF16 = jnp.bfloat16

RMS_EPS = 1e-6
HEAD_DIM = 128
PAGE = 128
POOL_WINDOWS = (2, 4, 8, 16)
POOL_HIST = max(POOL_WINDOWS) - 1

VMEM_LIMIT = 52 * 1024 * 1024

Q_TILE = 256
K_TILE = 256
HEADS_PER_STEP = 8
LOG2E = 1.4426950408889634
SOFTPLUS_CLAMP = 64.0
PAGES_PER_STEP = 4
HEAD_GROUP = 8
POOL_SUB = 128


def _silu(x):
    return x * (1.0 / (1.0 + jnp.exp(-x)))


def _split_bf16(x):
    hi = x.astype(BF16)
    lo = (x - hi.astype(F32)).astype(BF16)
    return hi, lo


def _params(*sem):
    return pltpu.CompilerParams(dimension_semantics=sem, vmem_limit_bytes=VMEM_LIMIT)


def _mod_kernel(c_ref, w_ref, b_ref, o_ref):
    s = _silu(c_ref[...]).astype(BF16)
    w = w_ref[...].astype(BF16)
    o_ref[...] = jnp.dot(s, w, preferred_element_type=F32) + b_ref[...]


def _modulation(c_all, w_mod, b_mod):
    depth, d, n = w_mod.shape
    rows = c_all.shape[0]
    tn = 1024
    return pl.pallas_call(
        _mod_kernel,
        grid=(depth, n // tn),
        in_specs=[
            pl.BlockSpec((rows, d), lambda l, j: (0, 0)),
            pl.BlockSpec((None, d, tn), lambda l, j: (l, 0, j)),
            pl.BlockSpec((None, 1, tn), lambda l, j: (l, 0, j)),
        ],
        out_specs=pl.BlockSpec((None, rows, tn), lambda l, j: (l, 0, j)),
        out_shape=jax.ShapeDtypeStruct((depth, rows, n), F32),
        compiler_params=_params("parallel", "parallel"),
        name="modulation",
    )(c_all, w_mod, b_mod.reshape(depth, 1, n))


def _modulated_norm(x_ref, shift_ref, scale_ref, g_ref, h_ref):
    x = x_ref[...]
    y = x * lax.rsqrt(jnp.mean(x * x, axis=-1, keepdims=True) + RMS_EPS)
    h = (y * g_ref[...]) * (1.0 + scale_ref[...]) + shift_ref[...]
    h_ref[...] = h.astype(BF16)


def _head_rmsnorm(a, g):
    return a * lax.rsqrt(jnp.mean(a * a, axis=-1, keepdims=True) + RMS_EPS) * g


def _proj_a_kernel(x_ref, shift_ref, scale_ref, g_ref, wq_ref, wk_ref, wv_ref, wg_ref,
                   gq_ref, gk_ref,
                   q_ref, kf_ref, kb_ref, vf_ref, vb_ref, gate_ref, h_ref, *, q_scale):
    @pl.when(pl.program_id(1) == 0)
    def _():
        _modulated_norm(x_ref, shift_ref, scale_ref, g_ref, h_ref)

    h = h_ref[...]
    n_heads = q_ref.shape[1] // HEAD_DIM
    q = jnp.dot(h, wq_ref[...], preferred_element_type=F32)
    k = jnp.dot(h, wk_ref[...], preferred_element_type=F32)
    for hd in range(n_heads):
        sl = slice(hd * HEAD_DIM, (hd + 1) * HEAD_DIM)
        qn = _head_rmsnorm(q[:, sl], gq_ref[...])
        q_ref[:, sl] = (qn * q_scale).astype(q_ref.dtype)
        kn = _head_rmsnorm(k[:, sl], gk_ref[...])
        kf_ref[:, sl] = kn
        kb_ref[:, sl] = kn.astype(kb_ref.dtype)
    v = jnp.dot(h, wv_ref[...], preferred_element_type=F32)
    vf_ref[...] = v
    vb_ref[...] = v.astype(vb_ref.dtype)
    gate_ref[...] = jnp.dot(h, wg_ref[...], preferred_element_type=F32)


def _mod_specs(mod, tm, d, rows_per_seq):
    if rows_per_seq is None:
        return [pl.BlockSpec((tm, d), functools.partial(lambda i, j, c: (i, c), c=c))
                for c in range(3)]
    tiles = rows_per_seq // tm
    return [pl.BlockSpec((None, 1, d),
                         functools.partial(lambda i, j, c: (i // tiles, 0, c), c=c))
            for c in range(3)]


def _proj_a(x, mod, rows_per_seq, norm_g, w4, gq, gk, small_dtype, tm, tn=512):
    rows, d = x.shape
    width = w4[0].shape[1]
    shift_spec, scale_spec, _ = _mod_specs(mod, tm, d, rows_per_seq)
    w_spec = pl.BlockSpec((d, tn), lambda i, j: (0, j))
    o_spec = pl.BlockSpec((tm, tn), lambda i, j: (i, j))
    row_spec = pl.BlockSpec((1, HEAD_DIM), lambda i, j: (0, 0))
    f32_out = jax.ShapeDtypeStruct((rows, width), F32)
    small_out = jax.ShapeDtypeStruct((rows, width), small_dtype)
    return pl.pallas_call(
        functools.partial(_proj_a_kernel, q_scale=HEAD_DIM ** -0.5 * LOG2E),
        grid=(rows // tm, width // tn),
        in_specs=[pl.BlockSpec((tm, d), lambda i, j: (i, 0)), shift_spec, scale_spec,
                  pl.BlockSpec((1, d), lambda i, j: (0, 0)),
                  w_spec, w_spec, w_spec, w_spec, row_spec, row_spec],
        out_specs=[o_spec] * 6,
        out_shape=[small_out, f32_out, small_out, f32_out, small_out, f32_out],
        scratch_shapes=[pltpu.VMEM((tm, d), BF16)],
        compiler_params=_params("parallel", "arbitrary"),
        name="proj_a",
    )(x, mod, mod, norm_g, *w4, gq, gk)


def _proj_b_kernel(x_ref, shift_ref, scale_ref, g_ref, wu_ref, wg_ref, u_ref, gate_ref, h_ref):
    @pl.when(pl.program_id(1) == 0)
    def _():
        _modulated_norm(x_ref, shift_ref, scale_ref, g_ref, h_ref)

    h = h_ref[...]
    u_ref[...] = jnp.dot(h, wu_ref[...], preferred_element_type=F32)
    gate_ref[...] = jnp.dot(h, wg_ref[...], preferred_element_type=F32)


def _proj_b(x, mod, rows_per_seq, norm_g, w2, tm, tn=512):
    rows, d = x.shape
    width = w2[0].shape[1]
    shift_spec, scale_spec, _ = _mod_specs(mod, tm, d, rows_per_seq)
    w_spec = pl.BlockSpec((d, tn), lambda i, j: (0, j))
    o_spec = pl.BlockSpec((tm, tn), lambda i, j: (i, j))
    f32_out = jax.ShapeDtypeStruct((rows, width), F32)
    return pl.pallas_call(
        _proj_b_kernel,
        grid=(rows // tm, width // tn),
        in_specs=[pl.BlockSpec((tm, d), lambda i, j: (i, 0)), shift_spec, scale_spec,
                  pl.BlockSpec((1, d), lambda i, j: (0, 0)), w_spec, w_spec],
        out_specs=[o_spec, o_spec],
        out_shape=[f32_out, f32_out],
        scratch_shapes=[pltpu.VMEM((tm, d), BF16)],
        compiler_params=_params("parallel", "arbitrary"),
        name="proj_b",
    )(x, mod, mod, norm_g, *w2)


def _out_kernel(inp_ref, w_ref, x_ref, gmod_ref, o_ref):
    y = jnp.dot(inp_ref[...].astype(BF16), w_ref[...], preferred_element_type=F32)
    o_ref[...] = x_ref[...] + gmod_ref[...] * y


def _out_proj(inp, w_out, x, mod, rows_per_seq, tm, tn=1024):
    rows, d = x.shape
    kdim = inp.shape[1]
    if rows_per_seq is None:
        gmod_spec = pl.BlockSpec((tm, tn), lambda i, j: (i, 2 * (d // tn) + j))
    else:
        tiles = rows_per_seq // tm
        gmod_spec = pl.BlockSpec((None, 1, tn), lambda i, j: (i // tiles, 0, 2 * (d // tn) + j))
    return pl.pallas_call(
        _out_kernel,
        grid=(rows // tm, d // tn),
        in_specs=[pl.BlockSpec((tm, kdim), lambda i, j: (i, 0)),
                  pl.BlockSpec((kdim, tn), lambda i, j: (0, j)),
                  pl.BlockSpec((tm, tn), lambda i, j: (i, j)),
                  gmod_spec],
        out_specs=pl.BlockSpec((tm, tn), lambda i, j: (i, j)),
        out_shape=jax.ShapeDtypeStruct((rows, d), F32),
        compiler_params=_params("parallel", "parallel"),
        name="out_proj",
    )(inp, w_out, x, mod)


def _sb_block(z, tri, mask, carry):
    log_beta, l, tot = _sb_survival(z, mask)
    later = jnp.dot(l, tri, preferred_element_type=F32)
    return _sb_weights(log_beta, later, carry, mask), tot


def _sb_survival(z, mask):
    sp = jnp.maximum(jnp.log2(1.0 + jnp.exp2(jnp.minimum(z, SOFTPLUS_CLAMP))), z)
    l = sp if mask is None else jnp.where(mask, sp, 0.0)
    return z - sp, l.astype(BF16), jnp.sum(l, axis=1, keepdims=True)


def _sb_weights(log_beta, later, carry, mask):
    logp = log_beta - later
    p = jnp.exp2(logp if carry is None else logp - carry)
    return p if mask is None else jnp.where(mask, p, 0.0)


def _attn_prompt_kernel(bias_ref, q_ref, k_ref, v_ref, gate_ref, tri_ref, o_ref,
                        acc_ref, carry_ref):
    hp = pl.program_id(1)
    qi = pl.program_id(2)
    tri = tri_ref[...]
    heads = [slice(hh * HEAD_DIM, (hh + 1) * HEAD_DIM) for hh in range(HEADS_PER_STEP)]

    def scores(kj, hh):
        ks = pl.multiple_of(kj * K_TILE, K_TILE)
        k = k_ref[pl.ds(ks, K_TILE), heads[hh]]
        v = v_ref[pl.ds(ks, K_TILE), heads[hh]]
        z = lax.dot_general(q_ref[:, heads[hh]], k, (((1,), (1,)), ((), ())),
                            preferred_element_type=F32)
        return z + bias_ref[hp * HEADS_PER_STEP + hh] * LOG2E, v

    def block(kj, mask):
        hs = range(HEADS_PER_STEP)
        first = mask is not None
        st = [dict() for _ in hs]

        def qk(hh):
            st[hh]["z"], st[hh]["v"] = scores(kj, hh)

        def survive(hh):
            s = st[hh]
            s["log_beta"], l, s["tot"] = _sb_survival(s.pop("z"), mask)
            s["later"] = jnp.dot(l, tri, preferred_element_type=F32)

        def weigh(hh):
            s = st[hh]
            carry = None
            if not first:
                s["carry"] = carry_ref[hh]
                carry = jnp.concatenate([s["carry"]] * (K_TILE // HEAD_DIM), axis=1)
            p = _sb_weights(s.pop("log_beta"), s.pop("later"), carry, mask)
            s["pv"] = jnp.dot(p.astype(BF16), s.pop("v"), preferred_element_type=F32)

        def accumulate(hh):
            s = st[hh]
            if first:
                acc_ref[hh] = s["pv"]
                carry_ref[hh] = jnp.broadcast_to(s["tot"], carry_ref.shape[1:])
            else:
                acc_ref[hh] += s["pv"]
                carry_ref[hh] = s["carry"] + s["tot"]

        for stage in (qk, survive, weigh, accumulate):
            for hh in hs:
                stage(hh)

    row = lax.broadcasted_iota(jnp.int32, (Q_TILE, K_TILE), 0)
    col = lax.broadcasted_iota(jnp.int32, (Q_TILE, K_TILE), 1)
    block(qi, col < row)

    def body(t, _):
        block(qi - 1 - t, None)
        return 0

    lax.fori_loop(0, qi, body, 0)
    for hh in range(HEADS_PER_STEP):
        o_ref[:, heads[hh]] = (acc_ref[hh] * _silu(gate_ref[:, heads[hh]])).astype(o_ref.dtype)


def _attn_prompt(q, k, v, gate, bias, tri, batch, seq):
    rows, width = q.shape
    n_heads = width // HEAD_DIM
    nq = seq // Q_TILE
    hw = HEADS_PER_STEP * HEAD_DIM
    q_spec = pl.BlockSpec((Q_TILE, hw), lambda b, h, i: (b * nq + i, h))
    kv_spec = pl.BlockSpec((seq, hw), lambda b, h, i: (b, h))
    acc = pltpu.VMEM((HEADS_PER_STEP, Q_TILE, HEAD_DIM), F32)
    return pl.pallas_call(
        _attn_prompt_kernel,
        grid=(batch, n_heads // HEADS_PER_STEP, nq),
        in_specs=[pl.BlockSpec(memory_space=pltpu.SMEM), q_spec, kv_spec, kv_spec, q_spec,
                  pl.BlockSpec((K_TILE, K_TILE), lambda b, h, i: (0, 0))],
        out_specs=q_spec,
        out_shape=jax.ShapeDtypeStruct((rows, width), BF16),
        scratch_shapes=[acc, acc],
        compiler_params=_params("parallel", "parallel", "arbitrary"),
        name="attn_prompt",
    )(bias, q, k, v, gate, tri)


def _attn_sample_kernel(pt_ref, q_ref, kn_ref, vn_ref, gate_ref, bias_ref, tri_ref, *rest,
                        n_heads):
    del pt_ref
    n_parts = PAGES_PER_STEP * (n_heads // HEAD_GROUP)
    k_pages, v_pages = rest[:n_parts], rest[n_parts:2 * n_parts]
    o_ref, qbd_ref, knew_ref, vnew_ref, acc_ref, carry_ref = rest[2 * n_parts:]
    g = pl.program_id(1)
    n_tok = q_ref.shape[0]
    n_chunks = n_heads // 2
    cw = 2 * HEAD_DIM
    rows = n_heads * n_tok

    def new_chunk(ref, c):
        return ref[:, c * cw:(c + 1) * cw].astype(BF16)

    def page_chunk(parts, c):
        n_groups = n_heads // HEAD_GROUP
        grp, local = divmod(2 * c, HEAD_GROUP)

        def head(r, hd):
            flat = r.reshape(PAGE * HEAD_GROUP, HEAD_DIM)
            return flat[pl.ds(hd, PAGE, stride=HEAD_GROUP), :]
        return jnp.concatenate(
            [jnp.concatenate([head(r, local), head(r, local + 1)], axis=1)
             for r in parts[grp::n_groups]], axis=0).astype(BF16)

    def attend(load_k, load_v, width, mask):
        z = [lax.dot_general(qbd_ref[c], load_k(c), (((1,), (1,)), ((), ())),
                             preferred_element_type=F32) for c in range(n_chunks)]
        z = jnp.concatenate(z, axis=0) + bias_ref[...] * LOG2E
        carry = carry_ref[...]
        p, tot = _sb_block(z, tri_ref[:width, :width], mask, carry[:, :1])
        p = p.astype(BF16)
        for c in range(n_chunks):
            rs = slice(c * 2 * n_tok, (c + 1) * 2 * n_tok)
            acc_ref[rs, :] += jnp.dot(p[rs], load_v(c), preferred_element_type=F32)
        carry_ref[...] = carry + tot

    @pl.when(g == 0)
    def _():
        acc_ref[...] = jnp.zeros_like(acc_ref)
        carry_ref[...] = jnp.zeros_like(carry_ref)
        zero = jnp.zeros((n_tok, HEAD_DIM), F32)
        for c in range(n_chunks):
            q0 = q_ref[:, (2 * c) * HEAD_DIM:(2 * c + 1) * HEAD_DIM]
            q1 = q_ref[:, (2 * c + 1) * HEAD_DIM:(2 * c + 2) * HEAD_DIM]
            top = jnp.concatenate([q0, zero], axis=1)
            bot = jnp.concatenate([zero, q1], axis=1)
            qbd_ref[c] = jnp.concatenate([top, bot], axis=0).astype(BF16)
        knew_ref[...] = jnp.zeros_like(knew_ref)
        vnew_ref[...] = jnp.zeros_like(vnew_ref)
        knew_ref[:n_tok, :] = kn_ref[...]
        vnew_ref[:n_tok, :] = vn_ref[...]
        tok = lax.broadcasted_iota(jnp.int32, (rows, PAGE), 0) % n_tok
        col = lax.broadcasted_iota(jnp.int32, (rows, PAGE), 1)
        attend(functools.partial(new_chunk, knew_ref), functools.partial(new_chunk, vnew_ref),
               PAGE, col < tok)

    attend(functools.partial(page_chunk, k_pages), functools.partial(page_chunk, v_pages),
           PAGES_PER_STEP * PAGE, None)

    @pl.when(g == pl.num_programs(1) - 1)
    def _():
        for hd in range(n_heads):
            c, hh = hd // 2, hd % 2
            r0 = (c * 2 + hh) * n_tok
            o = acc_ref[r0:r0 + n_tok, hh * HEAD_DIM:(hh + 1) * HEAD_DIM]
            sl = slice(hd * HEAD_DIM, (hd + 1) * HEAD_DIM)
            o_ref[:, sl] = o * _silu(gate_ref[:, sl])


def _attn_sample(page_table, q, k_new, v_new, gate, bias_rows, tri, cache_k, cache_v, layer):
    batch, n_pages = page_table.shape
    rows, width = q.shape
    n_tok = rows // batch
    n_heads = width // HEAD_DIM
    npg = PAGES_PER_STEP
    n_groups = n_pages // npg
    tok_spec = pl.BlockSpec((n_tok, width), lambda b, g, pt: (b, 0))

    def page_spec(p, grp):
        def index(b, g, pt):
            return (layer, pt[b, (n_groups - 1 - g) * npg + p], 0, grp, 0)
        return pl.BlockSpec((None, None, PAGE, HEAD_GROUP, HEAD_DIM), index)

    page_specs = [page_spec(p, grp) for p in range(npg) for grp in range(n_heads // HEAD_GROUP)]
    grid_spec = pltpu.PrefetchScalarGridSpec(
        num_scalar_prefetch=1,
        grid=(batch, n_groups),
        in_specs=[tok_spec, tok_spec, tok_spec, tok_spec,
                  pl.BlockSpec((n_heads * n_tok, 1), lambda b, g, pt: (0, 0)),
                  pl.BlockSpec((npg * PAGE, npg * PAGE), lambda b, g, pt: (0, 0))]
                 + page_specs + page_specs,
        out_specs=tok_spec,
        scratch_shapes=[pltpu.VMEM((n_heads // 2, 2 * n_tok, 2 * HEAD_DIM), BF16),
                        pltpu.VMEM((PAGE, width), F32),
                        pltpu.VMEM((PAGE, width), F32),
                        pltpu.VMEM((n_heads * n_tok, 2 * HEAD_DIM), F32),
                        pltpu.VMEM((n_heads * n_tok, HEAD_DIM), F32)],
    )
    return pl.pallas_call(
        functools.partial(_attn_sample_kernel, n_heads=n_heads),
        grid_spec=grid_spec,
        out_shape=jax.ShapeDtypeStruct((rows, width), F32),
        compiler_params=_params("parallel", "arbitrary"),
        name="attn_sample",
    )(page_table, q, k_new, v_new, gate, bias_rows, tri,
      *([cache_k] * len(page_specs)), *([cache_v] * len(page_specs)))


def _group_mix(mean, u, gate, w_grp, scale):
    diff = (mean - u).astype(BF16)
    return jnp.dot(diff, w_grp, preferred_element_type=F32) * scale * _silu(gate)


def _pool_prompt_kernel(u_ref, halo_ref, gate_ref, wgrp_ref, scale_ref, band_ref, o_ref,
                        hi_ref, lo_ref, *, tiles_per_seq):
    tm = u_ref.shape[0]
    gdim = wgrp_ref.shape[1]
    tile = pl.program_id(0) % tiles_per_seq
    halo = jnp.where(tile == 0, 0.0, halo_ref[...])
    hi, lo = _split_bf16(halo)
    hi_ref[:POOL_SUB, :] = hi
    lo_ref[:POOL_SUB, :] = lo
    hi, lo = _split_bf16(u_ref[...])
    hi_ref[POOL_SUB:, :] = hi
    lo_ref[POOL_SUB:, :] = lo
    for m in range(tm // POOL_SUB):
        r0 = m * POOL_SUB
        pos = tile * tm + r0 + lax.broadcasted_iota(jnp.int32, (POOL_SUB, 1), 0)
        for g, w in enumerate(POOL_WINDOWS):
            cs = slice(g * gdim, (g + 1) * gdim)
            band = band_ref[g]
            total = (jnp.dot(band, hi_ref[r0:r0 + 2 * POOL_SUB, cs], preferred_element_type=F32)
                     + jnp.dot(band, lo_ref[r0:r0 + 2 * POOL_SUB, cs], preferred_element_type=F32))
            inv_cnt = 1.0 / jnp.minimum(w, pos + 1).astype(F32)
            mixed = _group_mix(total * inv_cnt, u_ref[r0:r0 + POOL_SUB, cs],
                               gate_ref[r0:r0 + POOL_SUB, cs], wgrp_ref[g], scale_ref[:, cs])
            o_ref[r0:r0 + POOL_SUB, cs] = mixed.astype(o_ref.dtype)


def _pool_prompt(u, gate, w_grp, scale, bands, seq, tm=512):
    rows, width = u.shape
    sub_per_tile = tm // POOL_SUB
    tile_spec = pl.BlockSpec((tm, width), lambda i: (i, 0))
    return pl.pallas_call(
        functools.partial(_pool_prompt_kernel, tiles_per_seq=seq // tm),
        grid=(rows // tm,),
        in_specs=[tile_spec,
                  pl.BlockSpec((POOL_SUB, width), lambda i: (jnp.maximum(i * sub_per_tile - 1, 0), 0)),
                  tile_spec,
                  pl.BlockSpec(w_grp.shape, lambda i: (0, 0, 0)),
                  pl.BlockSpec((1, width), lambda i: (0, 0)),
                  pl.BlockSpec(bands.shape, lambda i: (0, 0, 0))],
        out_specs=tile_spec,
        out_shape=jax.ShapeDtypeStruct((rows, width), BF16),
        scratch_shapes=[pltpu.VMEM((tm + POOL_SUB, width), BF16),
                        pltpu.VMEM((tm + POOL_SUB, width), BF16)],
        compiler_params=_params("parallel"),
        name="pool_prompt",
    )(u, u, gate, w_grp, scale, bands)


def _pool_sample_kernel(ext_ref, gate_ref, wgrp_ref, scale_ref, o_ref):
    n_tok = gate_ref.shape[0]
    gdim = wgrp_ref.shape[1]
    last = ext_ref.shape[0] - n_tok
    for g, w in enumerate(POOL_WINDOWS):
        cs = slice(g * gdim, (g + 1) * gdim)
        u = ext_ref[last:last + n_tok, cs]
        total = u
        for back in range(1, w):
            total = total + ext_ref[last - back:last - back + n_tok, cs]
        o_ref[:, cs] = _group_mix(total / float(w), u, gate_ref[:, cs], wgrp_ref[g], scale_ref[:, cs])


def _pool_sample(ext, gate, w_grp, scale):
    batch, ext_rows, width = ext.shape
    rows = gate.shape[0]
    n_tok = rows // batch
    tok_spec = pl.BlockSpec((n_tok, width), lambda b: (b, 0))
    return pl.pallas_call(
        _pool_sample_kernel,
        grid=(batch,),
        in_specs=[pl.BlockSpec((None, ext_rows, width), lambda b: (b, 0, 0)),
                  tok_spec,
                  pl.BlockSpec(w_grp.shape, lambda b: (0, 0, 0)),
                  pl.BlockSpec((1, width), lambda b: (0, 0))],
        out_specs=tok_spec,
        out_shape=jax.ShapeDtypeStruct((rows, width), F32),
        compiler_params=_params("parallel"),
        name="pool_sample",
    )(ext, gate, w_grp, scale)


def _strict_lower(n):
    j = lax.broadcasted_iota(jnp.int32, (n, n), 0)
    s = lax.broadcasted_iota(jnp.int32, (n, n), 1)
    return (j > s).astype(BF16)


def _pool_bands():
    r = lax.broadcasted_iota(jnp.int32, (POOL_SUB, 2 * POOL_SUB), 0)
    c = lax.broadcasted_iota(jnp.int32, (POOL_SUB, 2 * POOL_SUB), 1)
    back = POOL_SUB + r - c
    return jnp.stack([((back >= 0) & (back < w)).astype(BF16) for w in POOL_WINDOWS])


def kernel(x_prompt, x_sample, cache_k, cache_v, state_pool, page_table, c_prompt, c_sample,
           norm_g, w_mod, b_mod, w_in_a, q_norm_g, k_norm_g, sb_bias, w_out_a,
           w_in_b, w_grp_b, scale_b, w_out_b):
    batch, seq, d = x_prompt.shape
    dec_batch, dec_seq, _ = x_sample.shape
    depth = norm_g.shape[0]
    n_heads = sb_bias.shape[1]
    width_a = n_heads * HEAD_DIM
    n_phys = cache_k.shape[1]
    rows_p, rows_s = batch * seq, dec_batch * dec_seq
    assert cache_k.shape[2] == PAGE and cache_k.shape[4] == HEAD_DIM
    assert state_pool.shape[2] == POOL_HIST and seq % 512 == 0

    n_seq = batch + dec_batch
    pad = (-n_seq) % 8
    c_all = jnp.concatenate([c_prompt, c_sample, jnp.zeros((pad, d), F32)], axis=0)
    mods = _modulation(c_all, w_mod, b_mod)

    tri_p = _strict_lower(K_TILE)
    tri_s = _strict_lower(PAGES_PER_STEP * PAGE)
    bands = _pool_bands()

    xp = x_prompt.reshape(rows_p, d)
    xs = x_sample.reshape(rows_s, d)
    nk_p, nv_p, nk_s, nv_s, npool_p, npool_s = [], [], [], [], [], []
    for i in range(depth):
        mod_p = mods[i, :batch].reshape(batch, 1, 3 * d)
        mod_s = jnp.repeat(mods[i, batch:n_seq], dec_seq, axis=0)
        g_i = norm_g[i].reshape(1, d)
        j = i // 2
        if i % 2 == 0:
            w4 = [w_in_a[j, :, s * width_a:(s + 1) * width_a].astype(BF16) for s in range(4)]
            w_out = w_out_a[j].astype(BF16)
            gq = q_norm_g[j].reshape(1, HEAD_DIM)
            gk = k_norm_g[j].reshape(1, HEAD_DIM)
            q_p, kf_p, kb_p, vf_p, vb_p, gate_p = _proj_a(
                xp, mod_p, seq, g_i, w4, gq, gk, BF16, tm=512)
            q_s, kf_s, _, vf_s, _, gate_s = _proj_a(
                xs, mod_s, None, g_i, w4, gq, gk, F32, tm=rows_s)
            o_p = _attn_prompt(q_p, kb_p, vb_p, gate_p, sb_bias[j], tri_p, batch, seq)
            bias_rows = jnp.repeat(sb_bias[j], dec_seq).reshape(n_heads * dec_seq, 1)
            o_s = _attn_sample(page_table, q_s, kf_s, vf_s, gate_s, bias_rows, tri_s,
                               cache_k, cache_v, j)
            nk_p.append(kf_p.reshape(batch, seq, n_heads, HEAD_DIM))
            nv_p.append(vf_p.reshape(batch, seq, n_heads, HEAD_DIM))
            nk_s.append(kf_s.reshape(dec_batch, dec_seq, n_heads, HEAD_DIM))
            nv_s.append(vf_s.reshape(dec_batch, dec_seq, n_heads, HEAD_DIM))
        else:
            width_b = w_in_b.shape[2] // 2
            w2 = [w_in_b[j, :, s * width_b:(s + 1) * width_b].astype(BF16) for s in range(2)]
            w_out = w_out_b[j].astype(BF16)
            w_grp = w_grp_b[j].astype(BF16)
            scale = scale_b[j].reshape(1, width_b)
            u_p, gate_p = _proj_b(xp, mod_p, seq, g_i, w2, tm=512)
            u_s, gate_s = _proj_b(xs, mod_s, None, g_i, w2, tm=rows_s)
            o_p = _pool_prompt(u_p, gate_p, w_grp, scale, bands, seq)
            ext = jnp.concatenate([state_pool[j], u_s.reshape(dec_batch, dec_seq, width_b)], axis=1)
            front = (-POOL_HIST) % 8
            ext_pad = jnp.pad(ext, ((0, 0), (front, 0), (0, 0)))
            o_s = _pool_sample(ext_pad, gate_s, w_grp, scale)
            npool_p.append(u_p.reshape(batch, seq, width_b)[:, seq - POOL_HIST:])
            npool_s.append(ext[:, dec_seq:])
        xp = _out_proj(o_p, w_out, xp, mod_p, seq, tm=512)
        xs = _out_proj(o_s, w_out, xs, mod_s, None, tm=rows_s)
    return (xp.reshape(batch, seq, d), xs.reshape(dec_batch, dec_seq, d),
            jnp.stack(nk_p), jnp.stack(nv_p), jnp.stack(nk_s), jnp.stack(nv_s),
            jnp.stack(npool_p), jnp.stack(npool_s))
```

```python
import functools

import jax
import jax.numpy as jnp
from jax import lax
from jax.experimental import pallas as pl
from jax.experimental.pallas import tpu as pltpu

F32 = jnp.float32
BF16 = jnp.bfloat16

RMS_EPS = 1e-6
HEAD_DIM = 128
PAGE = 128
POOL_WINDOWS = (2, 4, 8, 16)
POOL_HIST = max(POOL_WINDOWS) - 1

VMEM_LIMIT = 52 * 1024 * 1024

Q_TILE = 256
K_TILE = 256
HEADS_PER_STEP = 8
LOG2E = 1.4426950408889634
SOFTPLUS_CLAMP = 64.0
PAGES_PER_STEP = 4
HEAD_GROUP = 8
POOL_SUB = 128


def _silu(x):
    return x * (1.0 / (1.0 + jnp.exp(-x)))


def _split_bf16(x):
    hi = x.astype(BF16)
    lo = (x - hi.astype(F32)).astype(BF16)
    return hi, lo


def _params(*sem):
    return pltpu.CompilerParams(dimension_semantics=sem, vmem_limit_bytes=VMEM_LIMIT)


def _mod_kernel(c_ref, w_ref, b_ref, o_ref):
    s = _silu(c_ref[...]).astype(BF16)
    w = w_ref[...].astype(BF16)
    o_ref[...] = jnp.dot(s, w, preferred_element_type=F32) + b_ref[...]


def _modulation(c_all, w_mod, b_mod):
    depth, d, n = w_mod.shape
    rows = c_all.shape[0]
    tn = 1024
    return pl.pallas_call(
        _mod_kernel,
        grid=(depth, n // tn),
        in_specs=[
            pl.BlockSpec((rows, d), lambda l, j: (0, 0)),
            pl.BlockSpec((None, d, tn), lambda l, j: (l, 0, j)),
            pl.BlockSpec((None, 1, tn), lambda l, j: (l, 0, j)),
        ],
        out_specs=pl.BlockSpec((None, rows, tn), lambda l, j: (l, 0, j)),
        out_shape=jax.ShapeDtypeStruct((depth, rows, n), F32),
        compiler_params=_params("parallel", "parallel"),
        name="modulation",
    )(c_all, w_mod, b_mod.reshape(depth, 1, n))


def _modulated_norm(x_ref, shift_ref, scale_ref, g_ref, h_ref):
    x = x_ref[...]
    y = x * lax.rsqrt(jnp.mean(x * x, axis=-1, keepdims=True) + RMS_EPS)
    h = (y * g_ref[...]) * (1.0 + scale_ref[...]) + shift_ref[...]
    h_ref[...] = h.astype(BF16)


def _head_rmsnorm(a, g):
    return a * lax.rsqrt(jnp.mean(a * a, axis=-1, keepdims=True) + RMS_EPS) * g


def _proj_a_kernel(x_ref, shift_ref, scale_ref, g_ref, wq_ref, wk_ref, wv_ref, wg_ref,
                   gq_ref, gk_ref,
                   q_ref, kf_ref, kb_ref, vf_ref, vb_ref, gate_ref, h_ref, *, q_scale):
    @pl.when(pl.program_id(1) == 0)
    def _():
        _modulated_norm(x_ref, shift_ref, scale_ref, g_ref, h_ref)

    h = h_ref[...]
    n_heads = q_ref.shape[1] // HEAD_DIM
    q = jnp.dot(h, wq_ref[...], preferred_element_type=F32)
    k = jnp.dot(h, wk_ref[...], preferred_element_type=F32)
    for hd in range(n_heads):
        sl = slice(hd * HEAD_DIM, (hd + 1) * HEAD_DIM)
        qn = _head_rmsnorm(q[:, sl], gq_ref[...])
        q_ref[:, sl] = (qn * q_scale).astype(q_ref.dtype)
        kn = _head_rmsnorm(k[:, sl], gk_ref[...])
        kf_ref[:, sl] = kn
        kb_ref[:, sl] = kn.astype(kb_ref.dtype)
    v = jnp.dot(h, wv_ref[...], preferred_element_type=F32)
    vf_ref[...] = v
    vb_ref[...] = v.astype(vb_ref.dtype)
    gate_ref[...] = jnp.dot(h, wg_ref[...], preferred_element_type=F32)


def _mod_specs(mod, tm, d, rows_per_seq):
    if rows_per_seq is None:
        return [pl.BlockSpec((tm, d), functools.partial(lambda i, j, c: (i, c), c=c))
                for c in range(3)]
    tiles = rows_per_seq // tm
    return [pl.BlockSpec((None, 1, d),
                         functools.partial(lambda i, j, c: (i // tiles, 0, c), c=c))
            for c in range(3)]


def _section_specs(w_in, layer, n_sections, tn):
    d = w_in.shape[1]
    per_section = w_in.shape[2] // n_sections // tn
    return [pl.BlockSpec((None, d, tn),
                         functools.partial(lambda i, j, s: (layer, 0, s * per_section + j), s=s))
            for s in range(n_sections)]


def _proj_a(x, mod, rows_per_seq, norm_g, w_in, layer, gq, gk, small_dtype, tm, tn=256):
    rows, d = x.shape
    width = w_in.shape[2] // 4
    shift_spec, scale_spec, _ = _mod_specs(mod, tm, d, rows_per_seq)
    o_spec = pl.BlockSpec((tm, tn), lambda i, j: (i, j))
    row_spec = pl.BlockSpec((1, HEAD_DIM), lambda i, j: (0, 0))
    f32_out = jax.ShapeDtypeStruct((rows, width), F32)
    small_out = jax.ShapeDtypeStruct((rows, width), small_dtype)
    return pl.pallas_call(
        functools.partial(_proj_a_kernel, q_scale=HEAD_DIM ** -0.5 * LOG2E),
        grid=(rows // tm, width // tn),
        in_specs=[pl.BlockSpec((tm, d), lambda i, j: (i, 0)), shift_spec, scale_spec,
                  pl.BlockSpec((1, d), lambda i, j: (0, 0)),
                  *_section_specs(w_in, layer, 4, tn), row_spec, row_spec],
        out_specs=[o_spec] * 6,
        out_shape=[small_out, f32_out, small_out, f32_out, small_out, f32_out],
        scratch_shapes=[pltpu.VMEM((tm, d), BF16)],
        compiler_params=_params("parallel", "arbitrary"),
        name="proj_a",
    )(x, mod, mod, norm_g, w_in, w_in, w_in, w_in, gq, gk)


def _proj_b_kernel(x_ref, shift_ref, scale_ref, g_ref, wu_ref, wg_ref, u_ref, gate_ref, h_ref):
    @pl.when(pl.program_id(1) == 0)
    def _():
        _modulated_norm(x_ref, shift_ref, scale_ref, g_ref, h_ref)

    h = h_ref[...]
    u_ref[...] = jnp.dot(h, wu_ref[...], preferred_element_type=F32)
    gate_ref[...] = jnp.dot(h, wg_ref[...], preferred_element_type=F32)


def _proj_b(x, mod, rows_per_seq, norm_g, w_in, layer, tm, tn=512):
    rows, d = x.shape
    width = w_in.shape[2] // 2
    shift_spec, scale_spec, _ = _mod_specs(mod, tm, d, rows_per_seq)
    o_spec = pl.BlockSpec((tm, tn), lambda i, j: (i, j))
    f32_out = jax.ShapeDtypeStruct((rows, width), F32)
    return pl.pallas_call(
        _proj_b_kernel,
        grid=(rows // tm, width // tn),
        in_specs=[pl.BlockSpec((tm, d), lambda i, j: (i, 0)), shift_spec, scale_spec,
                  pl.BlockSpec((1, d), lambda i, j: (0, 0)),
                  *_section_specs(w_in, layer, 2, tn)],
        out_specs=[o_spec, o_spec],
        out_shape=[f32_out, f32_out],
        scratch_shapes=[pltpu.VMEM((tm, d), BF16)],
        compiler_params=_params("parallel", "arbitrary"),
        name="proj_b",
    )(x, mod, mod, norm_g, w_in, w_in)


def _out_kernel(inp_ref, w_ref, x_ref, gmod_ref, o_ref):
    y = jnp.dot(inp_ref[...].astype(BF16), w_ref[...], preferred_element_type=F32)
    o_ref[...] = x_ref[...] + gmod_ref[...] * y


def _out_proj(inp, w_out, layer, x, mod, rows_per_seq, tm):
    rows, d = x.shape
    tn = d
    kdim = inp.shape[1]
    if rows_per_seq is None:
        gmod_spec = pl.BlockSpec((tm, tn), lambda i, j: (i, 2 * (d // tn) + j))
    else:
        tiles = rows_per_seq // tm
        gmod_spec = pl.BlockSpec((None, 1, tn), lambda i, j: (i // tiles, 0, 2 * (d // tn) + j))
    return pl.pallas_call(
        _out_kernel,
        grid=(rows // tm, d // tn),
        in_specs=[pl.BlockSpec((tm, kdim), lambda i, j: (i, 0)),
                  pl.BlockSpec((None, kdim, tn), lambda i, j: (layer, 0, j)),
                  pl.BlockSpec((tm, tn), lambda i, j: (i, j)),
                  gmod_spec],
        out_specs=pl.BlockSpec((tm, tn), lambda i, j: (i, j)),
        out_shape=jax.ShapeDtypeStruct((rows, d), F32),
        compiler_params=_params("parallel", "parallel"),
        name="out_proj",
    )(inp, w_out, x, mod)


def _sb_block(z, tri, mask, carry):
    log_beta, l, tot = _sb_survival(z, mask)
    later = jnp.dot(l, tri, preferred_element_type=F32)
    return _sb_weights(log_beta, later, carry, mask), tot


def _sb_survival(z, mask):
    sp = jnp.maximum(jnp.log2(1.0 + jnp.exp2(jnp.minimum(z, SOFTPLUS_CLAMP))), z)
    l = sp if mask is None else jnp.where(mask, sp, 0.0)
    return z - sp, l.astype(BF16), jnp.sum(l, axis=1, keepdims=True)


def _sb_weights(log_beta, later, carry, mask):
    logp = log_beta - later
    p = jnp.exp2(logp if carry is None else logp - carry)
    return p if mask is None else jnp.where(mask, p, 0.0)


def _attn_prompt_kernel(bias_ref, q_ref, k_ref, v_ref, gate_ref, tri_ref, o_ref,
                        acc_ref, carry_ref):
    hp = pl.program_id(1)
    qi = pl.program_id(2)
    tri = tri_ref[...]
    heads = [slice(hh * HEAD_DIM, (hh + 1) * HEAD_DIM) for hh in range(HEADS_PER_STEP)]

    def scores(kj, hh):
        ks = pl.multiple_of(kj * K_TILE, K_TILE)
        k = k_ref[pl.ds(ks, K_TILE), heads[hh]]
        v = v_ref[pl.ds(ks, K_TILE), heads[hh]]
        z = lax.dot_general(q_ref[:, heads[hh]], k, (((1,), (1,)), ((), ())),
                            preferred_element_type=F32)
        return z + bias_ref[hp * HEADS_PER_STEP + hh] * LOG2E, v

    def block(kj, mask):
        hs = range(HEADS_PER_STEP)
        first = mask is not None
        st = [dict() for _ in hs]

        def qk(hh):
            st[hh]["z"], st[hh]["v"] = scores(kj, hh)

        def survive(hh):
            s = st[hh]
            s["log_beta"], l, s["tot"] = _sb_survival(s.pop("z"), mask)
            s["later"] = jnp.dot(l, tri, preferred_element_type=F32)

        def weigh(hh):
            s = st[hh]
            carry = None
            if not first:
                s["carry"] = carry_ref[hh]
                carry = jnp.concatenate([s["carry"]] * (K_TILE // HEAD_DIM), axis=1)
            p = _sb_weights(s.pop("log_beta"), s.pop("later"), carry, mask)
            s["pv"] = jnp.dot(p.astype(BF16), s.pop("v"), preferred_element_type=F32)

        def accumulate(hh):
            s = st[hh]
            if first:
                acc_ref[hh] = s["pv"]
                carry_ref[hh] = jnp.broadcast_to(s["tot"], carry_ref.shape[1:])
            else:
                acc_ref[hh] += s["pv"]
                carry_ref[hh] = s["carry"] + s["tot"]

        for stage in (qk, survive, weigh, accumulate):
            for hh in hs:
                stage(hh)

    row = lax.broadcasted_iota(jnp.int32, (Q_TILE, K_TILE), 0)
    col = lax.broadcasted_iota(jnp.int32, (Q_TILE, K_TILE), 1)
    block(qi, col < row)

    def body(t, _):
        block(qi - 1 - t, None)
        return 0

    lax.fori_loop(0, qi, body, 0)
    for hh in range(HEADS_PER_STEP):
        o_ref[:, heads[hh]] = (acc_ref[hh] * _silu(gate_ref[:, heads[hh]])).astype(o_ref.dtype)


def _attn_prompt(q, k, v, gate, bias, tri, batch, seq):
    rows, width = q.shape
    n_heads = width // HEAD_DIM
    nq = seq // Q_TILE
    hw = HEADS_PER_STEP * HEAD_DIM
    q_spec = pl.BlockSpec((Q_TILE, hw), lambda b, h, i: (b * nq + i, h))
    kv_spec = pl.BlockSpec((seq, hw), lambda b, h, i: (b, h))
    acc = pltpu.VMEM((HEADS_PER_STEP, Q_TILE, HEAD_DIM), F32)
    return pl.pallas_call(
        _attn_prompt_kernel,
        grid=(batch, n_heads // HEADS_PER_STEP, nq),
        in_specs=[pl.BlockSpec(memory_space=pltpu.SMEM), q_spec, kv_spec, kv_spec, q_spec,
                  pl.BlockSpec((K_TILE, K_TILE), lambda b, h, i: (0, 0))],
        out_specs=q_spec,
        out_shape=jax.ShapeDtypeStruct((rows, width), BF16),
        scratch_shapes=[acc, acc],
        compiler_params=_params("parallel", "parallel", "arbitrary"),
        name="attn_prompt",
    )(bias, q, k, v, gate, tri)


def _attn_sample_kernel(pt_ref, q_ref, kn_ref, vn_ref, gate_ref, bias_ref, tri_ref, *rest,
                        n_heads):
    del pt_ref
    n_parts = PAGES_PER_STEP * (n_heads // HEAD_GROUP)
    k_pages, v_pages = rest[:n_parts], rest[n_parts:2 * n_parts]
    o_ref, qbd_ref, knew_ref, vnew_ref, acc_ref, carry_ref = rest[2 * n_parts:]
    g = pl.program_id(1)
    n_tok = q_ref.shape[0]
    n_chunks = n_heads // 2
    cw = 2 * HEAD_DIM
    rows = n_heads * n_tok

    def new_chunk(ref, c):
        return ref[:, c * cw:(c + 1) * cw].astype(BF16)

    def page_chunk(parts, c):
        n_groups = n_heads // HEAD_GROUP
        grp, local = divmod(2 * c, HEAD_GROUP)

        def head(r, hd):
            flat = r.reshape(PAGE * HEAD_GROUP, HEAD_DIM)
            return flat[pl.ds(hd, PAGE, stride=HEAD_GROUP), :]
        return jnp.concatenate(
            [jnp.concatenate([head(r, local), head(r, local + 1)], axis=1)
             for r in parts[grp::n_groups]], axis=0).astype(BF16)

    def attend(load_k, load_v, width, mask):
        z = [lax.dot_general(qbd_ref[c], load_k(c), (((1,), (1,)), ((), ())),
                             preferred_element_type=F32) for c in range(n_chunks)]
        z = jnp.concatenate(z, axis=0) + bias_ref[...] * LOG2E
        carry = carry_ref[...]
        p, tot = _sb_block(z, tri_ref[:width, :width], mask, carry[:, :1])
        p = p.astype(BF16)
        for c in range(n_chunks):
            rs = slice(c * 2 * n_tok, (c + 1) * 2 * n_tok)
            acc_ref[rs, :] += jnp.dot(p[rs], load_v(c), preferred_element_type=F32)
        carry_ref[...] = carry + tot

    @pl.when(g == 0)
    def _():
        acc_ref[...] = jnp.zeros_like(acc_ref)
        carry_ref[...] = jnp.zeros_like(carry_ref)
        zero = jnp.zeros((n_tok, HEAD_DIM), F32)
        for c in range(n_chunks):
            q0 = q_ref[:, (2 * c) * HEAD_DIM:(2 * c + 1) * HEAD_DIM]
            q1 = q_ref[:, (2 * c + 1) * HEAD_DIM:(2 * c + 2) * HEAD_DIM]
            top = jnp.concatenate([q0, zero], axis=1)
            bot = jnp.concatenate([zero, q1], axis=1)
            qbd_ref[c] = jnp.concatenate([top, bot], axis=0).astype(BF16)
        knew_ref[...] = jnp.zeros_like(knew_ref)
        vnew_ref[...] = jnp.zeros_like(vnew_ref)
        knew_ref[:n_tok, :] = kn_ref[...]
        vnew_ref[:n_tok, :] = vn_ref[...]
        tok = lax.broadcasted_iota(jnp.int32, (rows, PAGE), 0) % n_tok
        col = lax.broadcasted_iota(jnp.int32, (rows, PAGE), 1)
        attend(functools.partial(new_chunk, knew_ref), functools.partial(new_chunk, vnew_ref),
               PAGE, col < tok)

    attend(functools.partial(page_chunk, k_pages), functools.partial(page_chunk, v_pages),
           PAGES_PER_STEP * PAGE, None)

    @pl.when(g == pl.num_programs(1) - 1)
    def _():
        for hd in range(n_heads):
            c, hh = hd // 2, hd % 2
            r0 = (c * 2 + hh) * n_tok
            o = acc_ref[r0:r0 + n_tok, hh * HEAD_DIM:(hh + 1) * HEAD_DIM]
            sl = slice(hd * HEAD_DIM, (hd + 1) * HEAD_DIM)
            o_ref[:, sl] = o * _silu(gate_ref[:, sl])


def _attn_sample(page_table, q, k_new, v_new, gate, bias_rows, tri, cache_k, cache_v, layer):
    batch, n_pages = page_table.shape
    rows, width = q.shape
    n_tok = rows // batch
    n_heads = width // HEAD_DIM
    npg = PAGES_PER_STEP
    n_groups = n_pages // npg
    tok_spec = pl.BlockSpec((n_tok, width), lambda b, g, pt: (b, 0))

    def page_spec(p, grp):
        def index(b, g, pt):
            return (layer, pt[b, (n_groups - 1 - g) * npg + p], 0, grp, 0)
        return pl.BlockSpec((None, None, PAGE, HEAD_GROUP, HEAD_DIM), index)

    page_specs = [page_spec(p, grp) for p in range(npg) for grp in range(n_heads // HEAD_GROUP)]
    grid_spec = pltpu.PrefetchScalarGridSpec(
        num_scalar_prefetch=1,
        grid=(batch, n_groups),
        in_specs=[tok_spec, tok_spec, tok_spec, tok_spec,
                  pl.BlockSpec((n_heads * n_tok, 1), lambda b, g, pt: (0, 0)),
                  pl.BlockSpec((npg * PAGE, npg * PAGE), lambda b, g, pt: (0, 0))]
                 + page_specs + page_specs,
        out_specs=tok_spec,
        scratch_shapes=[pltpu.VMEM((n_heads // 2, 2 * n_tok, 2 * HEAD_DIM), BF16),
                        pltpu.VMEM((PAGE, width), F32),
                        pltpu.VMEM((PAGE, width), F32),
                        pltpu.VMEM((n_heads * n_tok, 2 * HEAD_DIM), F32),
                        pltpu.VMEM((n_heads * n_tok, HEAD_DIM), F32)],
    )
    return pl.pallas_call(
        functools.partial(_attn_sample_kernel, n_heads=n_heads),
        grid_spec=grid_spec,
        out_shape=jax.ShapeDtypeStruct((rows, width), F32),
        compiler_params=_params("parallel", "arbitrary"),
        name="attn_sample",
    )(page_table, q, k_new, v_new, gate, bias_rows, tri,
      *([cache_k] * len(page_specs)), *([cache_v] * len(page_specs)))


def _group_mix(mean, u, gate, w_grp, scale):
    diff = (mean - u).astype(BF16)
    return jnp.dot(diff, w_grp, preferred_element_type=F32) * scale * _silu(gate)


def _pool_prompt_kernel(u_ref, halo_ref, gate_ref, wgrp_ref, scale_ref, band_ref, o_ref,
                        hi_ref, lo_ref, *, tiles_per_seq):
    tm = u_ref.shape[0]
    gdim = wgrp_ref.shape[1]
    tile = pl.program_id(0) % tiles_per_seq
    halo = jnp.where(tile == 0, 0.0, halo_ref[...])
    hi, lo = _split_bf16(halo)
    hi_ref[:POOL_SUB, :] = hi
    lo_ref[:POOL_SUB, :] = lo
    hi, lo = _split_bf16(u_ref[...])
    hi_ref[POOL_SUB:, :] = hi
    lo_ref[POOL_SUB:, :] = lo
    for m in range(tm // POOL_SUB):
        r0 = m * POOL_SUB
        pos = tile * tm + r0 + lax.broadcasted_iota(jnp.int32, (POOL_SUB, 1), 0)
        for g, w in enumerate(POOL_WINDOWS):
            cs = slice(g * gdim, (g + 1) * gdim)
            band = band_ref[g]
            total = (jnp.dot(band, hi_ref[r0:r0 + 2 * POOL_SUB, cs], preferred_element_type=F32)
                     + jnp.dot(band, lo_ref[r0:r0 + 2 * POOL_SUB, cs], preferred_element_type=F32))
            inv_cnt = 1.0 / jnp.minimum(w, pos + 1).astype(F32)
            mixed = _group_mix(total * inv_cnt, u_ref[r0:r0 + POOL_SUB, cs],
                               gate_ref[r0:r0 + POOL_SUB, cs], wgrp_ref[g], scale_ref[:, cs])
            o_ref[r0:r0 + POOL_SUB, cs] = mixed.astype(o_ref.dtype)


def _pool_prompt(u, gate, w_grp, scale, bands, seq, tm=512):
    rows, width = u.shape
    sub_per_tile = tm // POOL_SUB
    tile_spec = pl.BlockSpec((tm, width), lambda i: (i, 0))
    return pl.pallas_call(
        functools.partial(_pool_prompt_kernel, tiles_per_seq=seq // tm),
        grid=(rows // tm,),
        in_specs=[tile_spec,
                  pl.BlockSpec((POOL_SUB, width), lambda i: (jnp.maximum(i * sub_per_tile - 1, 0), 0)),
                  tile_spec,
                  pl.BlockSpec(w_grp.shape, lambda i: (0, 0, 0)),
                  pl.BlockSpec((1, width), lambda i: (0, 0)),
                  pl.BlockSpec(bands.shape, lambda i: (0, 0, 0))],
        out_specs=tile_spec,
        out_shape=jax.ShapeDtypeStruct((rows, width), BF16),
        scratch_shapes=[pltpu.VMEM((tm + POOL_SUB, width), BF16),
                        pltpu.VMEM((tm + POOL_SUB, width), BF16)],
        compiler_params=_params("parallel"),
        name="pool_prompt",
    )(u, u, gate, w_grp, scale, bands)


def _pool_sample_kernel(ext_ref, gate_ref, wgrp_ref, scale_ref, o_ref):
    n_tok = gate_ref.shape[0]
    gdim = wgrp_ref.shape[1]
    last = ext_ref.shape[0] - n_tok
    for g, w in enumerate(POOL_WINDOWS):
        cs = slice(g * gdim, (g + 1) * gdim)
        u = ext_ref[last:last + n_tok, cs]
        total = u
        for back in range(1, w):
            total = total + ext_ref[last - back:last - back + n_tok, cs]
        o_ref[:, cs] = _group_mix(total / float(w), u, gate_ref[:, cs], wgrp_ref[g], scale_ref[:, cs])


def _pool_sample(ext, gate, w_grp, scale):
    batch, ext_rows, width = ext.shape
    rows = gate.shape[0]
    n_tok = rows // batch
    tok_spec = pl.BlockSpec((n_tok, width), lambda b: (b, 0))
    return pl.pallas_call(
        _pool_sample_kernel,
        grid=(batch,),
        in_specs=[pl.BlockSpec((None, ext_rows, width), lambda b: (b, 0, 0)),
                  tok_spec,
                  pl.BlockSpec(w_grp.shape, lambda b: (0, 0, 0)),
                  pl.BlockSpec((1, width), lambda b: (0, 0))],
        out_specs=tok_spec,
        out_shape=jax.ShapeDtypeStruct((rows, width), F32),
        compiler_params=_params("parallel"),
        name="pool_sample",
    )(ext, gate, w_grp, scale)


def _strict_lower(n):
    j = lax.broadcasted_iota(jnp.int32, (n, n), 0)
    s = lax.broadcasted_iota(jnp.int32, (n, n), 1)
    return (j > s).astype(BF16)


def _pool_bands():
    r = lax.broadcasted_iota(jnp.int32, (POOL_SUB, 2 * POOL_SUB), 0)
    c = lax.broadcasted_iota(jnp.int32, (POOL_SUB, 2 * POOL_SUB), 1)
    back = POOL_SUB + r - c
    return jnp.stack([((back >= 0) & (back < w)).astype(BF16) for w in POOL_WINDOWS])


def kernel(x_prompt, x_sample, cache_k, cache_v, state_pool, page_table, c_prompt, c_sample,
           norm_g, w_mod, b_mod, w_in_a, q_norm_g, k_norm_g, sb_bias, w_out_a,
           w_in_b, w_grp_b, scale_b, w_out_b):
    batch, seq, d = x_prompt.shape
    dec_batch, dec_seq, _ = x_sample.shape
    depth = norm_g.shape[0]
    n_heads = sb_bias.shape[1]
    width_a = n_heads * HEAD_DIM
    n_phys = cache_k.shape[1]
    rows_p, rows_s = batch * seq, dec_batch * dec_seq
    assert cache_k.shape[2] == PAGE and cache_k.shape[4] == HEAD_DIM
    assert state_pool.shape[2] == POOL_HIST and seq % 512 == 0

    n_seq = batch + dec_batch
    pad = (-n_seq) % 8
    c_all = jnp.concatenate([c_prompt, c_sample, jnp.zeros((pad, d), F32)], axis=0)
    mods = _modulation(c_all, w_mod, b_mod)

    tri_p = _strict_lower(K_TILE)
    tri_s = _strict_lower(PAGES_PER_STEP * PAGE)
    bands = _pool_bands()

    xp = x_prompt.reshape(rows_p, d)
    xs = x_sample.reshape(rows_s, d)
    nk_p, nv_p, nk_s, nv_s, npool_p, npool_s = [], [], [], [], [], []
    w_in_a_bf, w_out_a_bf = w_in_a.astype(BF16), w_out_a.astype(BF16)
    w_in_b_bf, w_out_b_bf = w_in_b.astype(BF16), w_out_b.astype(BF16)
    for i in range(depth):
        mod_p =mods[i, :batch].reshape(batch, 1, 3 * d)
        mod_s = jnp.repeat(mods[i, batch:n_seq], dec_seq, axis=0)
        g_i = norm_g[i].reshape(1, d)
        j = i // 2
        if i % 2 == 0:
            w_out = w_out_a_bf
            gq = q_norm_g[j].reshape(1, HEAD_DIM)
            gk = k_norm_g[j].reshape(1, HEAD_DIM)
            q_p, kf_p, kb_p, vf_p, vb_p, gate_p = _proj_a(
                xp, mod_p, seq, g_i, w_in_a_bf, j, gq, gk, BF16, tm=1024)
            q_s, kf_s, _, vf_s, _, gate_s = _proj_a(
                xs, mod_s, None, g_i, w_in_a_bf, j, gq, gk, F32, tm=rows_s)
            o_p = _attn_prompt(q_p, kb_p, vb_p, gate_p, sb_bias[j], tri_p, batch, seq)
            bias_rows = jnp.repeat(sb_bias[j], dec_seq).reshape(n_heads * dec_seq, 1)
            o_s = _attn_sample(page_table, q_s, kf_s, vf_s, gate_s, bias_rows, tri_s,
                               cache_k, cache_v, j)
            nk_p.append(kf_p.reshape(batch, seq, n_heads, HEAD_DIM))
            nv_p.append(vf_p.reshape(batch, seq, n_heads, HEAD_DIM))
            nk_s.append(kf_s.reshape(dec_batch, dec_seq, n_heads, HEAD_DIM))
            nv_s.append(vf_s.reshape(dec_batch, dec_seq, n_heads, HEAD_DIM))
        else:
            width_b = w_in_b.shape[2] // 2
            w_out = w_out_b_bf
            w_grp = w_grp_b[j].astype(BF16)
            scale = scale_b[j].reshape(1, width_b)
            u_p, gate_p = _proj_b(xp, mod_p, seq, g_i, w_in_b_bf, j, tm=1024)
            u_s, gate_s = _proj_b(xs, mod_s, None, g_i, w_in_b_bf, j, tm=rows_s)
            o_p = _pool_prompt(u_p, gate_p, w_grp, scale, bands, seq)
            ext = jnp.concatenate([state_pool[j], u_s.reshape(dec_batch, dec_seq, width_b)], axis=1)
            front = (-POOL_HIST) % 8
            ext_pad = jnp.pad(ext, ((0, 0), (front, 0), (0, 0)))
            o_s = _pool_sample(ext_pad, gate_s, w_grp, scale)
            npool_p.append(u_p.reshape(batch, seq, width_b)[:, seq - POOL_HIST:])
            npool_s.append(ext[:, dec_seq:])
        xp = _out_proj(o_p, w_out, j, xp, mod_p, seq, tm=512)
        xs = _out_proj(o_s, w_out, j, xs, mod_s, None, tm=rows_s)
    return (xp.reshape(batch, seq, d), xs.reshape(dec_batch, dec_seq, d),
            jnp.stack(nk_p), jnp.stack(nv_p), jnp.stack(nk_s), jnp.stack(nv_s),
            jnp.stack(npool_p), jnp.stack(npool_s))
```

```python
import functools

import jax
import jax.numpy as jnp
from jax import lax
from jax.experimental import pallas as pl
from jax.experimental.pallas import tpu as pltpu

F32 = jnp.float32
BF16 = jnp.bfloat16

RMS_EPS = 1e-6
HEAD_DIM = 128
PAGE = 128
POOL_WINDOWS = (2, 4, 8, 16)
POOL_HIST = max(POOL_WINDOWS) - 1

VMEM_LIMIT = 52 * 1024 * 1024

Q_TILE = 256
K_TILE = 256
HEADS_PER_STEP = 8
LOG2E = 1.4426950408889634
SOFTPLUS_CLAMP = 64.0
MASKED_LOGIT = 1e30
PAGES_PER_STEP = 4
HEAD_GROUP = 8
POOL_SUB = 128


def _silu(x):
    return x * (1.0 / (1.0 + jnp.exp(-x)))


def _split_bf16(x):
    hi = x.astype(BF16)
    lo = (x - hi.astype(F32)).astype(BF16)
    return hi, lo


def _params(*sem):
    return pltpu.CompilerParams(dimension_semantics=sem, vmem_limit_bytes=VMEM_LIMIT)


def _mod_kernel(c_ref, w_ref, b_ref, o_ref):
    s = _silu(c_ref[...]).astype(BF16)
    w = w_ref[...].astype(BF16)
    o_ref[...] = jnp.dot(s, w, preferred_element_type=F32) + b_ref[...]


def _modulation(c_all, w_mod, b_mod):
    depth, d, n = w_mod.shape
    rows = c_all.shape[0]
    tn = 1024
    return pl.pallas_call(
        _mod_kernel,
        grid=(depth, n // tn),
        in_specs=[
            pl.BlockSpec((rows, d), lambda l, j: (0, 0)),
            pl.BlockSpec((None, d, tn), lambda l, j: (l, 0, j)),
            pl.BlockSpec((None, 1, tn), lambda l, j: (l, 0, j)),
        ],
        out_specs=pl.BlockSpec((None, rows, tn), lambda l, j: (l, 0, j)),
        out_shape=jax.ShapeDtypeStruct((depth, rows, n), F32),
        compiler_params=_params("parallel", "parallel"),
        name="modulation",
    )(c_all, w_mod, b_mod.reshape(depth, 1, n))


def _modulated_norm(x_ref, shift_ref, scale_ref, g_ref, h_ref):
    x = x_ref[...]
    y = x * lax.rsqrt(jnp.mean(x * x, axis=-1, keepdims=True) + RMS_EPS)
    h = (y * g_ref[...]) * (1.0 + scale_ref[...]) + shift_ref[...]
    h_ref[...] = h.astype(BF16)


def _head_rmsnorm(a, g):
    return a * lax.rsqrt(jnp.mean(a * a, axis=-1, keepdims=True) + RMS_EPS) * g


def _proj_a_kernel(x_ref, shift_ref, scale_ref, g_ref, wq_ref, wk_ref, wv_ref, wg_ref,
                   gq_ref, gk_ref,
                   q_ref, kf_ref, kb_ref, vf_ref, vb_ref, gate_ref, h_ref, *, q_scale):
    @pl.when(pl.program_id(1) == 0)
    def _():
        _modulated_norm(x_ref, shift_ref, scale_ref, g_ref, h_ref)

    h = h_ref[...]
    n_heads = q_ref.shape[1] // HEAD_DIM
    q = jnp.dot(h, wq_ref[...], preferred_element_type=F32)
    k = jnp.dot(h, wk_ref[...], preferred_element_type=F32)
    for hd in range(n_heads):
        sl = slice(hd * HEAD_DIM, (hd + 1) * HEAD_DIM)
        qn = _head_rmsnorm(q[:, sl], gq_ref[...])
        q_ref[:, sl] = (qn * q_scale).astype(q_ref.dtype)
        kn = _head_rmsnorm(k[:, sl], gk_ref[...])
        kf_ref[:, sl] = kn
        kb_ref[:, sl] = kn.astype(kb_ref.dtype)
    v = jnp.dot(h, wv_ref[...], preferred_element_type=F32)
    vf_ref[...] = v
    vb_ref[...] = v.astype(vb_ref.dtype)
    gate_ref[...] = jnp.dot(h, wg_ref[...], preferred_element_type=F32)


def _mod_specs(mod, tm, d, rows_per_seq):
    if rows_per_seq is None:
        return [pl.BlockSpec((tm, d), functools.partial(lambda i, j, c: (i, c), c=c))
                for c in range(3)]
    tiles = rows_per_seq // tm
    return [pl.BlockSpec((None, 1, d),
                         functools.partial(lambda i, j, c: (i // tiles, 0, c), c=c))
            for c in range(3)]


def _section_specs(w_in, layer, n_sections, tn):
    d = w_in.shape[1]
    per_section = w_in.shape[2] // n_sections // tn
    return [pl.BlockSpec((None, d, tn),
                         functools.partial(lambda i, j, s: (layer, 0, s * per_section + j), s=s))
            for s in range(n_sections)]


def _proj_a(x, mod, rows_per_seq, norm_g, w_in, layer, gq, gk, small_dtype, tm, tn=256):
    rows, d = x.shape
    width = w_in.shape[2] // 4
    shift_spec, scale_spec, _ = _mod_specs(mod, tm, d, rows_per_seq)
    o_spec = pl.BlockSpec((tm, tn), lambda i, j: (i, j))
    row_spec = pl.BlockSpec((1, HEAD_DIM), lambda i, j: (0, 0))
    f32_out = jax.ShapeDtypeStruct((rows, width), F32)
    small_out = jax.ShapeDtypeStruct((rows, width), small_dtype)
    return pl.pallas_call(
        functools.partial(_proj_a_kernel, q_scale=HEAD_DIM ** -0.5 * LOG2E),
        grid=(rows // tm, width // tn),
        in_specs=[pl.BlockSpec((tm, d), lambda i, j: (i, 0)), shift_spec, scale_spec,
                  pl.BlockSpec((1, d), lambda i, j: (0, 0)),
                  *_section_specs(w_in, layer, 4, tn), row_spec, row_spec],
        out_specs=[o_spec] * 6,
        out_shape=[small_out, f32_out, small_out, f32_out, small_out, f32_out],
        scratch_shapes=[pltpu.VMEM((tm, d), BF16)],
        compiler_params=_params("parallel", "arbitrary"),
        name="proj_a",
    )(x, mod, mod, norm_g, w_in, w_in, w_in, w_in, gq, gk)


def _proj_b_kernel(x_ref, shift_ref, scale_ref, g_ref, wu_ref, wg_ref, u_ref, gate_ref, h_ref):
    @pl.when(pl.program_id(1) == 0)
    def _():
        _modulated_norm(x_ref, shift_ref, scale_ref, g_ref, h_ref)

    h = h_ref[...]
    u_ref[...] = jnp.dot(h, wu_ref[...], preferred_element_type=F32)
    gate_ref[...] = jnp.dot(h, wg_ref[...], preferred_element_type=F32)


def _proj_b(x, mod, rows_per_seq, norm_g, w_in, layer, tm, tn=512):
    rows, d = x.shape
    width = w_in.shape[2] // 2
    shift_spec, scale_spec, _ = _mod_specs(mod, tm, d, rows_per_seq)
    o_spec = pl.BlockSpec((tm, tn), lambda i, j: (i, j))
    f32_out = jax.ShapeDtypeStruct((rows, width), F32)
    return pl.pallas_call(
        _proj_b_kernel,
        grid=(rows // tm, width // tn),
        in_specs=[pl.BlockSpec((tm, d), lambda i, j: (i, 0)), shift_spec, scale_spec,
                  pl.BlockSpec((1, d), lambda i, j: (0, 0)),
                  *_section_specs(w_in, layer, 2, tn)],
        out_specs=[o_spec, o_spec],
        out_shape=[f32_out, f32_out],
        scratch_shapes=[pltpu.VMEM((tm, d), BF16)],
        compiler_params=_params("parallel", "arbitrary"),
        name="proj_b",
    )(x, mod, mod, norm_g, w_in, w_in)


def _out_kernel(inp_ref, w_ref, x_ref, gmod_ref, o_ref):
    y = jnp.dot(inp_ref[...].astype(BF16), w_ref[...], preferred_element_type=F32)
    o_ref[...] = x_ref[...] + gmod_ref[...] * y


def _out_proj(inp, w_out, layer, x, mod, rows_per_seq, tm):
    rows, d = x.shape
    tn = d
    kdim = inp.shape[1]
    if rows_per_seq is None:
        gmod_spec = pl.BlockSpec((tm, tn), lambda i, j: (i, 2 * (d // tn) + j))
    else:
        tiles = rows_per_seq // tm
        gmod_spec = pl.BlockSpec((None, 1, tn), lambda i, j: (i // tiles, 0, 2 * (d // tn) + j))
    return pl.pallas_call(
        _out_kernel,
        grid=(rows // tm, d // tn),
        in_specs=[pl.BlockSpec((tm, kdim), lambda i, j: (i, 0)),
                  pl.BlockSpec((None, kdim, tn), lambda i, j: (layer, 0, j)),
                  pl.BlockSpec((tm, tn), lambda i, j: (i, j)),
                  gmod_spec],
        out_specs=pl.BlockSpec((tm, tn), lambda i, j: (i, j)),
        out_shape=jax.ShapeDtypeStruct((rows, d), F32),
        compiler_params=_params("parallel", "parallel"),
        name="out_proj",
    )(inp, w_out, x, mod)


def _sb_block(z, tri, mask, carry):
    log_beta, l, tot = _sb_survival(z, mask)
    later = jnp.dot(l, tri, preferred_element_type=F32)
    return _sb_weights(log_beta, later, carry, mask), tot


def _sb_survival(z, mask):
    sp = jnp.maximum(jnp.log2(1.0 + jnp.exp2(jnp.minimum(z, SOFTPLUS_CLAMP))), z)
    l = sp if mask is None else jnp.where(mask, sp, 0.0)
    return z - sp, l.astype(BF16), jnp.sum(l, axis=1, keepdims=True)


def _sb_weights(log_beta, later, carry, mask):
    logp = log_beta - later
    p = jnp.exp2(logp if carry is None else logp - carry)
    return p if mask is None else jnp.where(mask, p, 0.0)


def _attn_kernel(pt_ref, qi_ref, kj_ref,
                 bias_ref, q_ref, k_ref, v_ref, gate_ref, tri_ref,
                 qs_ref, kn_ref, vn_ref, gs_ref, bias_rows_ref, tri_s_ref, *rest,
                 n_heads, steps_per_group, sample_steps):
    del pt_ref
    n_parts = PAGES_PER_STEP * (n_heads // HEAD_GROUP)
    k_pages, v_pages = rest[:n_parts], rest[n_parts:2 * n_parts]
    (o_ref, os_ref, acc_ref, carry_ref,
     qbd_ref, knew_ref, vnew_ref, accs_ref, carrys_ref) = rest[2 * n_parts:]
    step = pl.program_id(0)
    t = step % steps_per_group
    head0 = (step // steps_per_group) % (n_heads // HEADS_PER_STEP) * HEADS_PER_STEP
    qi, kj = qi_ref[step], kj_ref[step]
    n_tok = qs_ref.shape[0]
    n_chunks = n_heads // 2
    cw = 2 * HEAD_DIM
    rows = n_heads * n_tok
    heads = [slice(hh * HEAD_DIM, (hh + 1) * HEAD_DIM) for hh in range(HEADS_PER_STEP)]

    def new_chunk(ref, c):
        return ref[:, c * cw:(c + 1) * cw].astype(BF16)

    def page_chunk(parts, c):
        n_groups = n_heads // HEAD_GROUP
        grp, local = divmod(2 * c, HEAD_GROUP)

        def head(r, hd):
            flat = r.reshape(PAGE * HEAD_GROUP, HEAD_DIM)
            return flat[pl.ds(hd, PAGE, stride=HEAD_GROUP), :]
        return jnp.concatenate(
            [jnp.concatenate([head(r, local), head(r, local + 1)], axis=1)
             for r in parts[grp::n_groups]], axis=0).astype(BF16)

    def sample_scores(load_k, c):
        return lax.dot_general(qbd_ref[c], load_k(c), (((1,), (1,)), ((), ())),
                               preferred_element_type=F32)

    def sample_weights(z_chunks, width, mask):
        z = jnp.concatenate(z_chunks, axis=0) + bias_rows_ref[...] * LOG2E
        carry = carrys_ref[...]
        p, tot = _sb_block(z, tri_s_ref[:width, :width], mask, carry[:, :1])
        carrys_ref[...] = carry + tot
        return p.astype(BF16)

    def sample_values(p, load_v, c):
        rs = slice(c * 2 * n_tok, (c + 1) * 2 * n_tok)
        accs_ref[rs, :] += jnp.dot(p[rs], load_v(c), preferred_element_type=F32)

    @pl.when(t == 0)
    def _():
        accs_ref[...] = jnp.zeros_like(accs_ref)
        carrys_ref[...] = jnp.zeros_like(carrys_ref)
        zero = jnp.zeros((n_tok, HEAD_DIM), F32)
        for c in range(n_chunks):
            q0 = qs_ref[:, (2 * c) * HEAD_DIM:(2 * c + 1) * HEAD_DIM]
            q1 = qs_ref[:, (2 * c + 1) * HEAD_DIM:(2 * c + 2) * HEAD_DIM]
            top = jnp.concatenate([q0, zero], axis=1)
            bot = jnp.concatenate([zero, q1], axis=1)
            qbd_ref[c] = jnp.concatenate([top, bot], axis=0).astype(BF16)
        knew_ref[...] = jnp.zeros_like(knew_ref)
        vnew_ref[...] = jnp.zeros_like(vnew_ref)
        knew_ref[:n_tok, :] = kn_ref[...]
        vnew_ref[:n_tok, :] = vn_ref[...]
        tok = lax.broadcasted_iota(jnp.int32, (rows, PAGE), 0) % n_tok
        col = lax.broadcasted_iota(jnp.int32, (rows, PAGE), 1)
        load_k = functools.partial(new_chunk, knew_ref)
        load_v = functools.partial(new_chunk, vnew_ref)
        p = sample_weights([sample_scores(load_k, c) for c in range(n_chunks)], PAGE, col < tok)
        for c in range(n_chunks):
            sample_values(p, load_v, c)

    @pl.when(kj == qi)
    def _():
        acc_ref[...] = jnp.zeros_like(acc_ref)
        carry_ref[...] = jnp.zeros_like(carry_ref)

    def block(with_sample):
        hs = range(HEADS_PER_STEP)
        tri = tri_ref[...]
        row = lax.broadcasted_iota(jnp.int32, (Q_TILE, K_TILE), 0)
        col = lax.broadcasted_iota(jnp.int32, (Q_TILE, K_TILE), 1)
        keep = col - row < (qi - kj) * K_TILE
        st = [dict() for _ in hs]
        load_k = functools.partial(page_chunk, k_pages)
        load_v = functools.partial(page_chunk, v_pages)
        zs = []

        def qk(hh):
            z = lax.dot_general(q_ref[:, heads[hh]], k_ref[:, heads[hh]], (((1,), (1,)), ((), ())),
                                preferred_element_type=F32)
            z = z + bias_ref[head0 + hh] * LOG2E
            st[hh]["z"] = jnp.where(keep, z, -MASKED_LOGIT)
            if with_sample:
                zs.append(sample_scores(load_k, hh))

        def survive(hh):
            s = st[hh]
            s["log_beta"], l, s["tot"] = _sb_survival(s.pop("z"), None)
            s["later"] = jnp.dot(l, tri, preferred_element_type=F32)

        def weigh(hh):
            s = st[hh]
            s["carry"] = carry_ref[hh]
            carry = jnp.concatenate([s["carry"]] * (K_TILE // HEAD_DIM), axis=1)
            p = _sb_weights(s.pop("log_beta"), s.pop("later"), carry, None)
            s["pv"] = jnp.dot(p.astype(BF16), v_ref[:, heads[hh]], preferred_element_type=F32)
            if with_sample:
                sample_values(st[0]["ps"], load_v, hh)

        def accumulate(hh):
            s = st[hh]
            acc_ref[hh] += s["pv"]
            carry_ref[hh] = s["carry"] + s["tot"]

        for hh in hs:
            qk(hh)
        for hh in hs:
            survive(hh)
        if with_sample:
            st[0]["ps"] = sample_weights(zs, PAGES_PER_STEP * PAGE, None)
        for hh in hs:
            weigh(hh)
        for hh in hs:
            accumulate(hh)

    pl.when(t < sample_steps)(functools.partial(block, True))
    pl.when(t >= sample_steps)(functools.partial(block, False))

    @pl.when(kj == 0)
    def _():
        for hh in range(HEADS_PER_STEP):
            o_ref[:, heads[hh]] = (acc_ref[hh] * _silu(gate_ref[:, heads[hh]])).astype(o_ref.dtype)

    @pl.when(t == sample_steps - 1)
    def _():
        for hd in range(n_heads):
            c, hh = hd // 2, hd % 2
            r0 = (c * 2 + hh) * n_tok
            o = accs_ref[r0:r0 + n_tok, hh * HEAD_DIM:(hh + 1) * HEAD_DIM]
            sl = slice(hd * HEAD_DIM, (hd + 1) * HEAD_DIM)
            os_ref[:, sl] = o * _silu(gs_ref[:, sl])


def _attention(q, k, v, gate, bias, tri, batch, seq,
               page_table, q_s, k_new, v_new, gate_s, bias_rows, tri_s, cache_k, cache_v, layer):
    rows, width = q.shape
    n_heads = width // HEAD_DIM
    nq = seq // Q_TILE
    head_groups = n_heads // HEADS_PER_STEP
    dec_batch, n_pages = page_table.shape
    rows_s = q_s.shape[0]
    n_tok = rows_s // dec_batch
    npg = PAGES_PER_STEP
    sample_steps = n_pages // npg
    walk = [(i, j) for i in range(nq) for j in range(i, -1, -1)]
    steps_per_group = len(walk)
    n_groups = batch * head_groups
    assert n_groups == dec_batch and sample_steps <= steps_per_group
    assert K_TILE == Q_TILE and HEADS_PER_STEP * 2 == n_heads
    qi_tab = jnp.asarray([w[0] for w in walk] * n_groups, jnp.int32)
    kj_tab = jnp.asarray([w[1] for w in walk] * n_groups, jnp.int32)
    hw = HEADS_PER_STEP * HEAD_DIM

    def group(s):
        g = s // steps_per_group
        return g // head_groups, g % head_groups

    def q_index(s, pt, qi, kj):
        b, hg = group(s)
        return (b * nq + qi[s], hg)

    def k_index(s, pt, qi, kj):
        b, hg = group(s)
        return (b * nq + kj[s], hg)

    q_spec = pl.BlockSpec((Q_TILE, hw), q_index)
    k_spec = pl.BlockSpec((K_TILE, hw), k_index)
    tok_spec = pl.BlockSpec((n_tok, width), lambda s, pt, qi, kj: (s // steps_per_group, 0))

    def page_spec(p, grp):
        def index(s, pt, qi, kj):
            g = jnp.minimum(s % steps_per_group, sample_steps - 1)
            return (layer, pt[s // steps_per_group, (sample_steps - 1 - g) * npg + p], 0, grp, 0)
        return pl.BlockSpec((None, None, PAGE, HEAD_GROUP, HEAD_DIM), index)

    page_specs = [page_spec(p, grp) for p in range(npg) for grp in range(n_heads // HEAD_GROUP)]
    const = lambda s, pt, qi, kj: (0, 0)
    acc = pltpu.VMEM((HEADS_PER_STEP, Q_TILE, HEAD_DIM), F32)
    grid_spec = pltpu.PrefetchScalarGridSpec(
        num_scalar_prefetch=3,
        grid=(n_groups * steps_per_group,),
        in_specs=[pl.BlockSpec(memory_space=pltpu.SMEM), q_spec, k_spec, k_spec, q_spec,
                  pl.BlockSpec((K_TILE, K_TILE), const),
                  tok_spec, tok_spec, tok_spec, tok_spec,
                  pl.BlockSpec((n_heads * n_tok, 1), const),
                  pl.BlockSpec((npg * PAGE, npg * PAGE), const)]
                 + page_specs + page_specs,
        out_specs=[q_spec, tok_spec],
        scratch_shapes=[acc, acc,
                        pltpu.VMEM((n_heads // 2, 2 * n_tok, 2 * HEAD_DIM), BF16),
                        pltpu.VMEM((PAGE, width), F32),
                        pltpu.VMEM((PAGE, width), F32),
                        pltpu.VMEM((n_heads * n_tok, 2 * HEAD_DIM), F32),
                        pltpu.VMEM((n_heads * n_tok, HEAD_DIM), F32)],
    )
    return pl.pallas_call(
        functools.partial(_attn_kernel, n_heads=n_heads, steps_per_group=steps_per_group,
                          sample_steps=sample_steps),
        grid_spec=grid_spec,
        out_shape=[jax.ShapeDtypeStruct((rows, width), BF16),
                   jax.ShapeDtypeStruct((rows_s, width), F32)],
        compiler_params=_params("arbitrary"),
        name="attention",
    )(page_table, qi_tab, kj_tab, bias, q, k, v, gate, tri,
      q_s, k_new, v_new, gate_s, bias_rows, tri_s,
      *([cache_k] * len(page_specs)), *([cache_v] * len(page_specs)))


def _group_mix(mean, u, gate, w_grp, scale):
    diff = (mean - u).astype(BF16)
    return jnp.dot(diff, w_grp, preferred_element_type=F32) * scale * _silu(gate)


def _pool_prompt_kernel(u_ref, halo_ref, gate_ref, wgrp_ref, scale_ref, band_ref, o_ref,
                        hi_ref, lo_ref, *, tiles_per_seq):
    tm = u_ref.shape[0]
    gdim = wgrp_ref.shape[1]
    tile = pl.program_id(0) % tiles_per_seq
    halo = jnp.where(tile == 0, 0.0, halo_ref[...])
    hi, lo = _split_bf16(halo)
    hi_ref[:POOL_SUB, :] = hi
    lo_ref[:POOL_SUB, :] = lo
    hi, lo = _split_bf16(u_ref[...])
    hi_ref[POOL_SUB:, :] = hi
    lo_ref[POOL_SUB:, :] = lo
    for m in range(tm // POOL_SUB):
        r0 = m * POOL_SUB
        pos = tile * tm + r0 + lax.broadcasted_iota(jnp.int32, (POOL_SUB, 1), 0)
        for g, w in enumerate(POOL_WINDOWS):
            cs = slice(g * gdim, (g + 1) * gdim)
            band = band_ref[g]
            total = (jnp.dot(band, hi_ref[r0:r0 + 2 * POOL_SUB, cs], preferred_element_type=F32)
                     + jnp.dot(band, lo_ref[r0:r0 + 2 * POOL_SUB, cs], preferred_element_type=F32))
            inv_cnt = 1.0 / jnp.minimum(w, pos + 1).astype(F32)
            mixed = _group_mix(total * inv_cnt, u_ref[r0:r0 + POOL_SUB, cs],
                               gate_ref[r0:r0 + POOL_SUB, cs], wgrp_ref[g], scale_ref[:, cs])
            o_ref[r0:r0 + POOL_SUB, cs] = mixed.astype(o_ref.dtype)


def _pool_prompt(u, gate, w_grp, scale, bands, seq, tm=512):
    rows, width = u.shape
    sub_per_tile = tm // POOL_SUB
    tile_spec = pl.BlockSpec((tm, width), lambda i: (i, 0))
    return pl.pallas_call(
        functools.partial(_pool_prompt_kernel, tiles_per_seq=seq // tm),
        grid=(rows // tm,),
        in_specs=[tile_spec,
                  pl.BlockSpec((POOL_SUB, width), lambda i: (jnp.maximum(i * sub_per_tile - 1, 0), 0)),
                  tile_spec,
                  pl.BlockSpec(w_grp.shape, lambda i: (0, 0, 0)),
                  pl.BlockSpec((1, width), lambda i: (0, 0)),
                  pl.BlockSpec(bands.shape, lambda i: (0, 0, 0))],
        out_specs=tile_spec,
        out_shape=jax.ShapeDtypeStruct((rows, width), BF16),
        scratch_shapes=[pltpu.VMEM((tm + POOL_SUB, width), BF16),
                        pltpu.VMEM((tm + POOL_SUB, width), BF16)],
        compiler_params=_params("parallel"),
        name="pool_prompt",
    )(u, u, gate, w_grp, scale, bands)


def _pool_sample_kernel(ext_ref, gate_ref, wgrp_ref, scale_ref, o_ref):
    n_tok = gate_ref.shape[0]
    gdim = wgrp_ref.shape[1]
    last = ext_ref.shape[0] - n_tok
    for g, w in enumerate(POOL_WINDOWS):
        cs = slice(g * gdim, (g + 1) * gdim)
        u = ext_ref[last:last + n_tok, cs]
        total = u
        for back in range(1, w):
            total = total + ext_ref[last - back:last - back + n_tok, cs]
        o_ref[:, cs] = _group_mix(total / float(w), u, gate_ref[:, cs], wgrp_ref[g], scale_ref[:, cs])


def _pool_sample(ext, gate, w_grp, scale):
    batch, ext_rows, width = ext.shape
    rows = gate.shape[0]
    n_tok = rows // batch
    tok_spec = pl.BlockSpec((n_tok, width), lambda b: (b, 0))
    return pl.pallas_call(
        _pool_sample_kernel,
        grid=(batch,),
        in_specs=[pl.BlockSpec((None, ext_rows, width), lambda b: (b, 0, 0)),
                  tok_spec,
                  pl.BlockSpec(w_grp.shape, lambda b: (0, 0, 0)),
                  pl.BlockSpec((1, width), lambda b: (0, 0))],
        out_specs=tok_spec,
        out_shape=jax.ShapeDtypeStruct((rows, width), F32),
        compiler_params=_params("parallel"),
        name="pool_sample",
    )(ext, gate, w_grp, scale)


def _strict_lower(n):
    j = lax.broadcasted_iota(jnp.int32, (n, n), 0)
    s = lax.broadcasted_iota(jnp.int32, (n, n), 1)
    return (j > s).astype(BF16)


def _pool_bands():
    r = lax.broadcasted_iota(jnp.int32, (POOL_SUB, 2 * POOL_SUB), 0)
    c = lax.broadcasted_iota(jnp.int32, (POOL_SUB, 2 * POOL_SUB), 1)
    back = POOL_SUB + r - c
    return jnp.stack([((back >= 0) & (back < w)).astype(BF16) for w in POOL_WINDOWS])


def kernel(x_prompt, x_sample, cache_k, cache_v, state_pool, page_table, c_prompt, c_sample,
           norm_g, w_mod, b_mod, w_in_a, q_norm_g, k_norm_g, sb_bias, w_out_a,
           w_in_b, w_grp_b, scale_b, w_out_b):
    batch, seq, d = x_prompt.shape
    dec_batch, dec_seq, _ = x_sample.shape
    depth = norm_g.shape[0]
    n_heads = sb_bias.shape[1]
    width_a = n_heads * HEAD_DIM
    n_phys = cache_k.shape[1]
    rows_p, rows_s = batch * seq, dec_batch * dec_seq
    assert cache_k.shape[2] == PAGE and cache_k.shape[4] == HEAD_DIM
    assert state_pool.shape[2] == POOL_HIST and seq % 512 == 0

    n_seq = batch + dec_batch
    pad = (-n_seq) % 8
    c_all = jnp.concatenate([c_prompt, c_sample, jnp.zeros((pad, d), F32)], axis=0)
    mods = _modulation(c_all, w_mod, b_mod)

    tri_p = _strict_lower(K_TILE)
    tri_s = _strict_lower(PAGES_PER_STEP * PAGE)
    bands = _pool_bands()

    xp = x_prompt.reshape(rows_p, d)
    xs = x_sample.reshape(rows_s, d)
    nk_p, nv_p, nk_s, nv_s, npool_p, npool_s = [], [], [], [], [], []
    w_in_a_bf, w_out_a_bf = w_in_a.astype(BF16), w_out_a.astype(BF16)
    w_in_b_bf, w_out_b_bf = w_in_b.astype(BF16), w_out_b.astype(BF16)
    for i in range(depth):
        mod_p =mods[i, :batch].reshape(batch, 1, 3 * d)
        mod_s = jnp.repeat(mods[i, batch:n_seq], dec_seq, axis=0)
        g_i = norm_g[i].reshape(1, d)
        j = i // 2
        if i % 2 == 0:
            w_out = w_out_a_bf
            gq = q_norm_g[j].reshape(1, HEAD_DIM)
            gk = k_norm_g[j].reshape(1, HEAD_DIM)
            q_p, kf_p, kb_p, vf_p, vb_p, gate_p = _proj_a(
                xp, mod_p, seq, g_i, w_in_a_bf, j, gq, gk, BF16, tm=1024)
            q_s, kf_s, _, vf_s, _, gate_s = _proj_a(
                xs, mod_s, None, g_i, w_in_a_bf, j, gq, gk, F32, tm=rows_s)
            bias_rows = jnp.repeat(sb_bias[j], dec_seq).reshape(n_heads * dec_seq, 1)
            o_p, o_s = _attention(q_p, kb_p, vb_p, gate_p, sb_bias[j], tri_p, batch, seq,
                                  page_table, q_s, kf_s, vf_s, gate_s, bias_rows, tri_s,
                                  cache_k, cache_v, j)
            nk_p.append(kf_p.reshape(batch, seq, n_heads, HEAD_DIM))
            nv_p.append(vf_p.reshape(batch, seq, n_heads, HEAD_DIM))
            nk_s.append(kf_s.reshape(dec_batch, dec_seq, n_heads, HEAD_DIM))
            nv_s.append(vf_s.reshape(dec_batch, dec_seq, n_heads, HEAD_DIM))
        else:
            width_b = w_in_b.shape[2] // 2
            w_out = w_out_b_bf
            w_grp = w_grp_b[j].astype(BF16)
            scale = scale_b[j].reshape(1, width_b)
            u_p, gate_p = _proj_b(xp, mod_p, seq, g_i, w_in_b_bf, j, tm=1024)
            u_s, gate_s = _proj_b(xs, mod_s, None, g_i, w_in_b_bf, j, tm=rows_s)
            o_p = _pool_prompt(u_p, gate_p, w_grp, scale, bands, seq)
            ext = jnp.concatenate([state_pool[j], u_s.reshape(dec_batch, dec_seq, width_b)], axis=1)
            front = (-POOL_HIST) % 8
            ext_pad = jnp.pad(ext, ((0, 0), (front, 0), (0, 0)))
            o_s = _pool_sample(ext_pad, gate_s, w_grp, scale)
            npool_p.append(u_p.reshape(batch, seq, width_b)[:, seq - POOL_HIST:])
            npool_s.append(ext[:, dec_seq:])
        xp = _out_proj(o_p, w_out, j, xp, mod_p, seq, tm=512)
        xs = _out_proj(o_s, w_out, j, xs, mod_s, None, tm=rows_s)
    return (xp.reshape(batch, seq, d), xs.reshape(dec_batch, dec_seq, d),
            jnp.stack(nk_p), jnp.stack(nv_p), jnp.stack(nk_s), jnp.stack(nv_s),
            jnp.stack(npool_p), jnp.stack(npool_s))
```

```python
import functools

import jax
import jax.numpy as jnp
from jax import lax
from jax.experimental import pallas as pl
from jax.experimental.pallas import tpu as pltpu

F32 = jnp.float32
BF16 = jnp.bfloat16

RMS_EPS = 1e-6
HEAD_DIM = 128
PAGE = 128
POOL_WINDOWS = (2, 4, 8, 16)
POOL_HIST = max(POOL_WINDOWS) - 1

VMEM_LIMIT = 52 * 1024 * 1024

Q_TILE = 256
K_TILE = 256
HEADS_PER_STEP = 8
LOG2E = 1.4426950408889634
SOFTPLUS_CLAMP = 64.0
MASKED_LOGIT = 1e30
PAGES_PER_STEP = 4
HEAD_GROUP = 8
POOL_SUB = 128


def _silu(x):
    return x * (1.0 / (1.0 + jnp.exp(-x)))


def _split_bf16(x):
    hi = x.astype(BF16)
    lo = (x - hi.astype(F32)).astype(BF16)
    return hi, lo


def _params(*sem):
    return pltpu.CompilerParams(dimension_semantics=sem, vmem_limit_bytes=VMEM_LIMIT)


def _mod_kernel(c_ref, w_ref, b_ref, o_ref):
    s = _silu(c_ref[...]).astype(BF16)
    w = w_ref[...].astype(BF16)
    o_ref[...] = jnp.dot(s, w, preferred_element_type=F32) + b_ref[...]


def _modulation(c_all, w_mod, b_mod):
    depth, d, n = w_mod.shape
    rows = c_all.shape[0]
    tn = 1024
    return pl.pallas_call(
        _mod_kernel,
        grid=(depth, n // tn),
        in_specs=[
            pl.BlockSpec((rows, d), lambda l, j: (0, 0)),
            pl.BlockSpec((None, d, tn), lambda l, j: (l, 0, j)),
            pl.BlockSpec((None, 1, tn), lambda l, j: (l, 0, j)),
        ],
        out_specs=pl.BlockSpec((None, rows, tn), lambda l, j: (l, 0, j)),
        out_shape=jax.ShapeDtypeStruct((depth, rows, n), F32),
        compiler_params=_params("parallel", "parallel"),
        name="modulation",
    )(c_all, w_mod, b_mod.reshape(depth, 1, n))


def _modulated_norm(x_ref, shift_ref, scale_ref, g_ref, h_ref):
    x = x_ref[...]
    y = x * lax.rsqrt(jnp.mean(x * x, axis=-1, keepdims=True) + RMS_EPS)
    h = (y * g_ref[...]) * (1.0 + scale_ref[...]) + shift_ref[...]
    h_ref[...] = h.astype(BF16)


def _head_rmsnorm(a, g):
    return a * lax.rsqrt(jnp.mean(a * a, axis=-1, keepdims=True) + RMS_EPS) * g


def _mxu_weight(w_ref, copy_ref):
    w = w_ref[...].astype(BF16)
    if copy_ref is not None:
        copy_ref[...] = w
    return w


def _proj_a_kernel(x_ref, shift_ref, scale_ref, g_ref, wq_ref, wk_ref, wv_ref, wg_ref,
                   gq_ref, gk_ref, *rest, q_scale, n_prev, emit_weights):
    prev = rest[:2 * n_prev]
    q_ref, kf_ref, kb_ref, vf_ref, vb_ref, gate_ref = rest[2 * n_prev:2 * n_prev + 6]
    copies = rest[2 * n_prev + 6:-1] if emit_weights else (None,) * 4
    h_ref = rest[-1]

    @pl.when(pl.program_id(1) == 0)
    def _():
        _modulated_norm(x_ref, shift_ref, scale_ref, g_ref, h_ref)

    for layer in range(n_prev):
        kf_ref[layer] = prev[2 * layer][...]
        vf_ref[layer] = prev[2 * layer + 1][...]
    h = h_ref[...]
    n_heads = q_ref.shape[1] // HEAD_DIM
    q = jnp.dot(h, _mxu_weight(wq_ref, copies[0]), preferred_element_type=F32)
    k = jnp.dot(h, _mxu_weight(wk_ref, copies[1]), preferred_element_type=F32)
    for hd in range(n_heads):
        sl = slice(hd * HEAD_DIM, (hd + 1) * HEAD_DIM)
        qn = _head_rmsnorm(q[:, sl], gq_ref[...])
        q_ref[:, sl] = (qn * q_scale).astype(q_ref.dtype)
        kn = _head_rmsnorm(k[:, sl], gk_ref[...])
        kf_ref[n_prev, :, sl] = kn
        kb_ref[:, sl] = kn.astype(kb_ref.dtype)
    v = jnp.dot(h, _mxu_weight(wv_ref, copies[2]), preferred_element_type=F32)
    vf_ref[n_prev] = v
    vb_ref[...] = v.astype(vb_ref.dtype)
    gate_ref[...] = jnp.dot(h, _mxu_weight(wg_ref, copies[3]), preferred_element_type=F32)


def _mod_specs(mod, tm, d, rows_per_seq):
    if rows_per_seq is None:
        return [pl.BlockSpec((tm, d), functools.partial(lambda i, j, c: (i, c), c=c))
                for c in range(3)]
    tiles = rows_per_seq // tm
    return [pl.BlockSpec((None, 1, d),
                         functools.partial(lambda i, j, c: (i // tiles, 0, c), c=c))
            for c in range(3)]


def _section_specs(w_in, layer, n_sections, tn):
    d = w_in.shape[1]
    per_section = w_in.shape[2] // n_sections // tn
    return [pl.BlockSpec((None, d, tn),
                         functools.partial(lambda i, j, s: (layer, 0, s * per_section + j), s=s))
            for s in range(n_sections)]


def _weight_operands(w, layer, n_sections, tn):
    if layer is None:
        d, width = w[0].shape
        return list(w), [pl.BlockSpec((d, tn), lambda i, j: (0, j))] * n_sections, [], []
    d, width = w.shape[1], w.shape[2] // n_sections
    copy_spec = pl.BlockSpec((d, tn), lambda i, j: (0, j))
    return ([w] * n_sections, _section_specs(w, layer, n_sections, tn),
            [copy_spec] * n_sections, [jax.ShapeDtypeStruct((d, width), BF16)] * n_sections)


def _proj_a(x, mod, rows_per_seq, norm_g, w, layer, gq, gk, small_dtype, tm, tn=256, prev_kv=()):
    rows, d = x.shape
    w_args, w_specs, copy_specs, copy_shapes = _weight_operands(w, layer, 4, tn)
    width = w_args[0].shape[-1] // (1 if layer is None else 4)
    n_prev = len(prev_kv) // 2
    shift_spec, scale_spec, _ = _mod_specs(mod, tm, d, rows_per_seq)
    o_spec = pl.BlockSpec((tm, tn), lambda i, j: (i, j))
    kv_spec = pl.BlockSpec((n_prev + 1, tm, tn), lambda i, j: (0, i, j))
    row_spec = pl.BlockSpec((1, HEAD_DIM), lambda i, j: (0, 0))
    f32_out = jax.ShapeDtypeStruct((rows, width), F32)
    kv_out = jax.ShapeDtypeStruct((n_prev + 1, rows, width), F32)
    small_out = jax.ShapeDtypeStruct((rows, width), small_dtype)
    return pl.pallas_call(
        functools.partial(_proj_a_kernel, q_scale=HEAD_DIM ** -0.5 * LOG2E, n_prev=n_prev,
                          emit_weights=layer is not None),
        grid=(rows // tm, width // tn),
        in_specs=[pl.BlockSpec((tm, d), lambda i, j: (i, 0)), shift_spec, scale_spec,
                  pl.BlockSpec((1, d), lambda i, j: (0, 0)),
                  *w_specs, row_spec, row_spec, *([o_spec] * len(prev_kv))],
        out_specs=[o_spec, kv_spec, o_spec, kv_spec, o_spec, o_spec, *copy_specs],
        out_shape=[small_out, kv_out, small_out, kv_out, small_out, f32_out, *copy_shapes],
        scratch_shapes=[pltpu.VMEM((tm, d), BF16)],
        compiler_params=_params("parallel", "arbitrary"),
        name="proj_a",
    )(x, mod, mod, norm_g, *w_args, gq, gk, *prev_kv)


def _proj_b_kernel(x_ref, shift_ref, scale_ref, g_ref, wu_ref, wg_ref, u_ref, gate_ref, *rest,
                   emit_weights):
    copies = rest[:-1] if emit_weights else (None,) * 2
    h_ref = rest[-1]

    @pl.when(pl.program_id(1) == 0)
    def _():
        _modulated_norm(x_ref, shift_ref, scale_ref, g_ref, h_ref)

    h = h_ref[...]
    u_ref[...] = jnp.dot(h, _mxu_weight(wu_ref, copies[0]), preferred_element_type=F32)
    gate_ref[...] = jnp.dot(h, _mxu_weight(wg_ref, copies[1]), preferred_element_type=F32)


def _proj_b(x, mod, rows_per_seq, norm_g, w, layer, tm, tn=512):
    rows, d = x.shape
    w_args, w_specs, copy_specs, copy_shapes = _weight_operands(w, layer, 2, tn)
    width = w_args[0].shape[-1] // (1 if layer is None else 2)
    shift_spec, scale_spec, _ = _mod_specs(mod, tm, d, rows_per_seq)
    o_spec = pl.BlockSpec((tm, tn), lambda i, j: (i, j))
    f32_out = jax.ShapeDtypeStruct((rows, width), F32)
    return pl.pallas_call(
        functools.partial(_proj_b_kernel, emit_weights=layer is not None),
        grid=(rows // tm, width // tn),
        in_specs=[pl.BlockSpec((tm, d), lambda i, j: (i, 0)), shift_spec, scale_spec,
                  pl.BlockSpec((1, d), lambda i, j: (0, 0)), *w_specs],
        out_specs=[o_spec, o_spec, *copy_specs],
        out_shape=[f32_out, f32_out, *copy_shapes],
        scratch_shapes=[pltpu.VMEM((tm, d), BF16)],
        compiler_params=_params("parallel", "arbitrary"),
        name="proj_b",
    )(x, mod, mod, norm_g, *w_args)


def _out_kernel(inp_ref, w_ref, x_ref, gmod_ref, o_ref, *copy):
    w = _mxu_weight(w_ref, copy[0] if copy else None)
    y = jnp.dot(inp_ref[...].astype(BF16), w, preferred_element_type=F32)
    o_ref[...] = x_ref[...] + gmod_ref[...] * y


def _out_proj(inp, w, layer, x, mod, rows_per_seq, tm, tn):
    rows, d = x.shape
    kdim = inp.shape[1]
    w_args, w_specs, copy_specs, copy_shapes = _weight_operands(w, layer, 1, tn)
    if rows_per_seq is None:
        gmod_spec = pl.BlockSpec((tm, tn), lambda i, j: (i, 2 * (d // tn) + j))
    else:
        tiles = rows_per_seq // tm
        gmod_spec = pl.BlockSpec((None, 1, tn), lambda i, j: (i // tiles, 0, 2 * (d // tn) + j))
    o_spec = pl.BlockSpec((tm, tn), lambda i, j: (i, j))
    outs = pl.pallas_call(
        _out_kernel,
        grid=(rows // tm, d // tn),
        in_specs=[pl.BlockSpec((tm, kdim), lambda i, j: (i, 0)), *w_specs, o_spec, gmod_spec],
        out_specs=[o_spec, *copy_specs],
        out_shape=[jax.ShapeDtypeStruct((rows, d), F32), *copy_shapes],
        compiler_params=_params("parallel", "parallel"),
        name="out_proj",
    )(inp, *w_args, x, mod)
    return outs if copy_shapes else outs[0]


def _sb_block(z, tri, mask, carry):
    log_beta, l, tot = _sb_survival(z, mask)
    later = jnp.dot(l, tri, preferred_element_type=F32)
    return _sb_weights(log_beta, later, carry, mask), tot


def _sb_survival(z, mask):
    sp = jnp.maximum(jnp.log2(1.0 + jnp.exp2(jnp.minimum(z, SOFTPLUS_CLAMP))), z)
    l = sp if mask is None else jnp.where(mask, sp, 0.0)
    return z - sp, l.astype(BF16), jnp.sum(l, axis=1, keepdims=True)


def _sb_weights(log_beta, later, carry, mask):
    logp = log_beta - later
    p = jnp.exp2(logp if carry is None else logp - carry)
    return p if mask is None else jnp.where(mask, p, 0.0)


def _attn_kernel(pt_ref, qi_ref, kj_ref,
                 bias_ref, q_ref, k_ref, v_ref, gate_ref, tri_ref,
                 qs_ref, kn_ref, vn_ref, gs_ref, bias_rows_ref, tri_s_ref, *rest,
                 n_heads, steps_per_group, sample_steps):
    del pt_ref
    n_parts = PAGES_PER_STEP * (n_heads // HEAD_GROUP)
    k_pages, v_pages = rest[:n_parts], rest[n_parts:2 * n_parts]
    (o_ref, os_ref, acc_ref, carry_ref,
     qbd_ref, knew_ref, vnew_ref, accs_ref, carrys_ref) = rest[2 * n_parts:]
    step = pl.program_id(0)
    t = step % steps_per_group
    head0 = (step // steps_per_group) % (n_heads // HEADS_PER_STEP) * HEADS_PER_STEP
    qi, kj = qi_ref[step], kj_ref[step]
    n_tok = qs_ref.shape[0]
    n_chunks = n_heads // 2
    cw = 2 * HEAD_DIM
    rows = n_heads * n_tok
    heads = [slice(hh * HEAD_DIM, (hh + 1) * HEAD_DIM) for hh in range(HEADS_PER_STEP)]

    def new_chunk(ref, c):
        return ref[:, c * cw:(c + 1) * cw].astype(BF16)

    def page_chunk(parts, c):
        n_groups = n_heads // HEAD_GROUP
        grp, local = divmod(2 * c, HEAD_GROUP)

        def head(r, hd):
            flat = r.reshape(PAGE * HEAD_GROUP, HEAD_DIM)
            return flat[pl.ds(hd, PAGE, stride=HEAD_GROUP), :]
        return jnp.concatenate(
            [jnp.concatenate([head(r, local), head(r, local + 1)], axis=1)
             for r in parts[grp::n_groups]], axis=0).astype(BF16)

    def sample_scores(load_k, c):
        return lax.dot_general(qbd_ref[c], load_k(c), (((1,), (1,)), ((), ())),
                               preferred_element_type=F32)

    def sample_weights(z_chunks, width, mask):
        z = jnp.concatenate(z_chunks, axis=0) + bias_rows_ref[...] * LOG2E
        carry = carrys_ref[...]
        p, tot = _sb_block(z, tri_s_ref[:width, :width], mask, carry[:, :1])
        carrys_ref[...] = carry + tot
        return p.astype(BF16)

    def sample_values(p, load_v, c):
        rs = slice(c * 2 * n_tok, (c + 1) * 2 * n_tok)
        accs_ref[rs, :] += jnp.dot(p[rs], load_v(c), preferred_element_type=F32)

    @pl.when(t == 0)
    def _():
        accs_ref[...] = jnp.zeros_like(accs_ref)
        carrys_ref[...] = jnp.zeros_like(carrys_ref)
        zero = jnp.zeros((n_tok, HEAD_DIM), F32)
        for c in range(n_chunks):
            q0 = qs_ref[:, (2 * c) * HEAD_DIM:(2 * c + 1) * HEAD_DIM]
            q1 = qs_ref[:, (2 * c + 1) * HEAD_DIM:(2 * c + 2) * HEAD_DIM]
            top = jnp.concatenate([q0, zero], axis=1)
            bot = jnp.concatenate([zero, q1], axis=1)
            qbd_ref[c] = jnp.concatenate([top, bot], axis=0).astype(BF16)
        knew_ref[...] = jnp.zeros_like(knew_ref)
        vnew_ref[...] = jnp.zeros_like(vnew_ref)
        knew_ref[:n_tok, :] = kn_ref[...]
        vnew_ref[:n_tok, :] = vn_ref[...]
        tok = lax.broadcasted_iota(jnp.int32, (rows, PAGE), 0) % n_tok
        col = lax.broadcasted_iota(jnp.int32, (rows, PAGE), 1)
        load_k = functools.partial(new_chunk, knew_ref)
        load_v = functools.partial(new_chunk, vnew_ref)
        p = sample_weights([sample_scores(load_k, c) for c in range(n_chunks)], PAGE, col < tok)
        for c in range(n_chunks):
            sample_values(p, load_v, c)

    @pl.when(kj == qi)
    def _():
        acc_ref[...] = jnp.zeros_like(acc_ref)
        carry_ref[...] = jnp.zeros_like(carry_ref)

    def block(with_sample, diagonal):
        hs = range(HEADS_PER_STEP)
        tri = tri_ref[...]
        if diagonal:
            row = lax.broadcasted_iota(jnp.int32, (Q_TILE, K_TILE), 0)
            col = lax.broadcasted_iota(jnp.int32, (Q_TILE, K_TILE), 1)
            keep = col < row
        st = [dict() for _ in hs]
        load_k = functools.partial(page_chunk, k_pages)
        load_v = functools.partial(page_chunk, v_pages)
        zs = []

        def qk(hh):
            z = lax.dot_general(q_ref[:, heads[hh]], k_ref[:, heads[hh]], (((1,), (1,)), ((), ())),
                                preferred_element_type=F32)
            z = z + bias_ref[head0 + hh] * LOG2E
            st[hh]["z"] = jnp.where(keep, z, -MASKED_LOGIT) if diagonal else z
            if with_sample:
                zs.append(sample_scores(load_k, hh))

        def survive(hh):
            s = st[hh]
            s["log_beta"], l, s["tot"] = _sb_survival(s.pop("z"), None)
            s["later"] = jnp.dot(l, tri, preferred_element_type=F32)

        def weigh(hh):
            s = st[hh]
            s["carry"] = carry_ref[hh]
            carry = jnp.concatenate([s["carry"]] * (K_TILE // HEAD_DIM), axis=1)
            p = _sb_weights(s.pop("log_beta"), s.pop("later"), carry, None)
            s["pv"] = jnp.dot(p.astype(BF16), v_ref[:, heads[hh]], preferred_element_type=F32)
            if with_sample:
                sample_values(st[0]["ps"], load_v, hh)

        def accumulate(hh):
            s = st[hh]
            acc_ref[hh] += s["pv"]
            carry_ref[hh] = s["carry"] + s["tot"]

        for hh in hs:
            qk(hh)
        for hh in hs:
            survive(hh)
        if with_sample:
            st[0]["ps"] = sample_weights(zs, PAGES_PER_STEP * PAGE, None)
        for hh in hs:
            weigh(hh)
        for hh in hs:
            accumulate(hh)

    for with_sample in (True, False):
        for diagonal in (True, False):
            pl.when(((t < sample_steps) == with_sample) & ((kj == qi) == diagonal))(
                functools.partial(block, with_sample, diagonal))

    @pl.when(kj == 0)
    def _():
        for hh in range(HEADS_PER_STEP):
            o_ref[:, heads[hh]] = (acc_ref[hh] * _silu(gate_ref[:, heads[hh]])).astype(o_ref.dtype)

    @pl.when(t == sample_steps - 1)
    def _():
        for hd in range(n_heads):
            c, hh = hd // 2, hd % 2
            r0 = (c * 2 + hh) * n_tok
            o = accs_ref[r0:r0 + n_tok, hh * HEAD_DIM:(hh + 1) * HEAD_DIM]
            sl = slice(hd * HEAD_DIM, (hd + 1) * HEAD_DIM)
            os_ref[:, sl] = o * _silu(gs_ref[:, sl])


def _attention(q, k, v, gate, bias, tri, batch, seq,
               page_table, q_s, k_new, v_new, gate_s, bias_rows, tri_s, cache_k, cache_v, layer):
    rows, width = q.shape
    n_heads = width // HEAD_DIM
    nq = seq // Q_TILE
    head_groups = n_heads // HEADS_PER_STEP
    dec_batch, n_pages = page_table.shape
    rows_s = q_s.shape[0]
    n_tok = rows_s // dec_batch
    npg = PAGES_PER_STEP
    sample_steps = n_pages // npg
    walk = [(i, j) for i in range(nq) for j in range(i, -1, -1)]
    steps_per_group = len(walk)
    n_groups = batch * head_groups
    assert n_groups == dec_batch and sample_steps <= steps_per_group
    assert K_TILE == Q_TILE and HEADS_PER_STEP * 2 == n_heads
    qi_tab = jnp.asarray([w[0] for w in walk] * n_groups, jnp.int32)
    kj_tab = jnp.asarray([w[1] for w in walk] * n_groups, jnp.int32)
    hw = HEADS_PER_STEP * HEAD_DIM

    def group(s):
        g = s // steps_per_group
        return g // head_groups, g % head_groups

    def q_index(s, pt, qi, kj):
        b, hg = group(s)
        return (b * nq + qi[s], hg)

    def k_index(s, pt, qi, kj):
        b, hg = group(s)
        return (b * nq + kj[s], hg)

    q_spec = pl.BlockSpec((Q_TILE, hw), q_index)
    k_spec = pl.BlockSpec((K_TILE, hw), k_index)
    tok_spec = pl.BlockSpec((n_tok, width), lambda s, pt, qi, kj: (s // steps_per_group, 0))

    def page_spec(p, grp):
        def index(s, pt, qi, kj):
            g = jnp.minimum(s % steps_per_group, sample_steps - 1)
            return (layer, pt[s // steps_per_group, (sample_steps - 1 - g) * npg + p], 0, grp, 0)
        return pl.BlockSpec((None, None, PAGE, HEAD_GROUP, HEAD_DIM), index)

    page_specs = [page_spec(p, grp) for p in range(npg) for grp in range(n_heads // HEAD_GROUP)]
    const = lambda s, pt, qi, kj: (0, 0)
    acc = pltpu.VMEM((HEADS_PER_STEP, Q_TILE, HEAD_DIM), F32)
    grid_spec = pltpu.PrefetchScalarGridSpec(
        num_scalar_prefetch=3,
        grid=(n_groups * steps_per_group,),
        in_specs=[pl.BlockSpec(memory_space=pltpu.SMEM), q_spec, k_spec, k_spec, q_spec,
                  pl.BlockSpec((K_TILE, K_TILE), const),
                  tok_spec, tok_spec, tok_spec, tok_spec,
                  pl.BlockSpec((n_heads * n_tok, 1), const),
                  pl.BlockSpec((npg * PAGE, npg * PAGE), const)]
                 + page_specs + page_specs,
        out_specs=[q_spec, tok_spec],
        scratch_shapes=[acc, acc,
                        pltpu.VMEM((n_heads // 2, 2 * n_tok, 2 * HEAD_DIM), BF16),
                        pltpu.VMEM((PAGE, width), F32),
                        pltpu.VMEM((PAGE, width), F32),
                        pltpu.VMEM((n_heads * n_tok, 2 * HEAD_DIM), F32),
                        pltpu.VMEM((n_heads * n_tok, HEAD_DIM), F32)],
    )
    return pl.pallas_call(
        functools.partial(_attn_kernel, n_heads=n_heads, steps_per_group=steps_per_group,
                          sample_steps=sample_steps),
        grid_spec=grid_spec,
        out_shape=[jax.ShapeDtypeStruct((rows, width), BF16),
                   jax.ShapeDtypeStruct((rows_s, width), F32)],
        compiler_params=_params("arbitrary"),
        name="attention",
    )(page_table, qi_tab, kj_tab, bias, q, k, v, gate, tri,
      q_s, k_new, v_new, gate_s, bias_rows, tri_s,
      *([cache_k] * len(page_specs)), *([cache_v] * len(page_specs)))


def _group_mix(mean, u, gate, w_grp, scale):
    diff = (mean - u).astype(BF16)
    return jnp.dot(diff, w_grp, preferred_element_type=F32) * scale * _silu(gate)


def _pool_prompt_kernel(u_ref, halo_ref, gate_ref, wgrp_ref, scale_ref, band_ref, o_ref,
                        hi_ref, lo_ref, *, tiles_per_seq):
    tm = u_ref.shape[0]
    gdim = wgrp_ref.shape[1]
    tile = pl.program_id(0) % tiles_per_seq
    halo = jnp.where(tile == 0, 0.0, halo_ref[...])
    hi, lo = _split_bf16(halo)
    hi_ref[:POOL_SUB, :] = hi
    lo_ref[:POOL_SUB, :] = lo
    hi, lo = _split_bf16(u_ref[...])
    hi_ref[POOL_SUB:, :] = hi
    lo_ref[POOL_SUB:, :] = lo
    for m in range(tm // POOL_SUB):
        r0 = m * POOL_SUB
        pos = tile * tm + r0 + lax.broadcasted_iota(jnp.int32, (POOL_SUB, 1), 0)
        for g, w in enumerate(POOL_WINDOWS):
            cs = slice(g * gdim, (g + 1) * gdim)
            band = band_ref[g]
            total = (jnp.dot(band, hi_ref[r0:r0 + 2 * POOL_SUB, cs], preferred_element_type=F32)
                     + jnp.dot(band, lo_ref[r0:r0 + 2 * POOL_SUB, cs], preferred_element_type=F32))
            inv_cnt = 1.0 / jnp.minimum(w, pos + 1).astype(F32)
            mixed = _group_mix(total * inv_cnt, u_ref[r0:r0 + POOL_SUB, cs],
                               gate_ref[r0:r0 + POOL_SUB, cs], wgrp_ref[g], scale_ref[:, cs])
            o_ref[r0:r0 + POOL_SUB, cs] = mixed.astype(o_ref.dtype)


def _pool_prompt(u, gate, w_grp, scale, bands, seq, tm=512):
    rows, width = u.shape
    sub_per_tile = tm // POOL_SUB
    tile_spec = pl.BlockSpec((tm, width), lambda i: (i, 0))
    return pl.pallas_call(
        functools.partial(_pool_prompt_kernel, tiles_per_seq=seq // tm),
        grid=(rows // tm,),
        in_specs=[tile_spec,
                  pl.BlockSpec((POOL_SUB, width), lambda i: (jnp.maximum(i * sub_per_tile - 1, 0), 0)),
                  tile_spec,
                  pl.BlockSpec(w_grp.shape, lambda i: (0, 0, 0)),
                  pl.BlockSpec((1, width), lambda i: (0, 0)),
                  pl.BlockSpec(bands.shape, lambda i: (0, 0, 0))],
        out_specs=tile_spec,
        out_shape=jax.ShapeDtypeStruct((rows, width), BF16),
        scratch_shapes=[pltpu.VMEM((tm + POOL_SUB, width), BF16),
                        pltpu.VMEM((tm + POOL_SUB, width), BF16)],
        compiler_params=_params("parallel"),
        name="pool_prompt",
    )(u, u, gate, w_grp, scale, bands)


def _pool_sample_kernel(ext_ref, gate_ref, wgrp_ref, scale_ref, o_ref):
    n_tok = gate_ref.shape[0]
    gdim = wgrp_ref.shape[1]
    last = ext_ref.shape[0] - n_tok
    for g, w in enumerate(POOL_WINDOWS):
        cs = slice(g * gdim, (g + 1) * gdim)
        u = ext_ref[last:last + n_tok, cs]
        total = u
        for back in range(1, w):
            total = total + ext_ref[last - back:last - back + n_tok, cs]
        o_ref[:, cs] = _group_mix(total / float(w), u, gate_ref[:, cs], wgrp_ref[g], scale_ref[:, cs])


def _pool_sample(ext, gate, w_grp, scale):
    batch, ext_rows, width = ext.shape
    rows = gate.shape[0]
    n_tok = rows // batch
    tok_spec = pl.BlockSpec((n_tok, width), lambda b: (b, 0))
    return pl.pallas_call(
        _pool_sample_kernel,
        grid=(batch,),
        in_specs=[pl.BlockSpec((None, ext_rows, width), lambda b: (b, 0, 0)),
                  tok_spec,
                  pl.BlockSpec(w_grp.shape, lambda b: (0, 0, 0)),
                  pl.BlockSpec((1, width), lambda b: (0, 0))],
        out_specs=tok_spec,
        out_shape=jax.ShapeDtypeStruct((rows, width), F32),
        compiler_params=_params("parallel"),
        name="pool_sample",
    )(ext, gate, w_grp, scale)


def _strict_lower(n):
    j = lax.broadcasted_iota(jnp.int32, (n, n), 0)
    s = lax.broadcasted_iota(jnp.int32, (n, n), 1)
    return (j > s).astype(BF16)


def _pool_bands():
    r = lax.broadcasted_iota(jnp.int32, (POOL_SUB, 2 * POOL_SUB), 0)
    c = lax.broadcasted_iota(jnp.int32, (POOL_SUB, 2 * POOL_SUB), 1)
    back = POOL_SUB + r - c
    return jnp.stack([((back >= 0) & (back < w)).astype(BF16) for w in POOL_WINDOWS])


def kernel(x_prompt, x_sample, cache_k, cache_v, state_pool, page_table, c_prompt, c_sample,
           norm_g, w_mod, b_mod, w_in_a, q_norm_g, k_norm_g, sb_bias, w_out_a,
           w_in_b, w_grp_b, scale_b, w_out_b):
    batch, seq, d = x_prompt.shape
    dec_batch, dec_seq, _ = x_sample.shape
    depth = norm_g.shape[0]
    n_heads = sb_bias.shape[1]
    width_a = n_heads * HEAD_DIM
    n_phys = cache_k.shape[1]
    rows_p, rows_s = batch * seq, dec_batch * dec_seq
    assert cache_k.shape[2] == PAGE and cache_k.shape[4] == HEAD_DIM
    assert state_pool.shape[2] == POOL_HIST and seq % 512 == 0

    n_seq = batch + dec_batch
    pad = (-n_seq) % 8
    c_all = jnp.concatenate([c_prompt, c_sample, jnp.zeros((pad, d), F32)], axis=0)
    mods = _modulation(c_all, w_mod, b_mod)

    tri_p = _strict_lower(K_TILE)
    tri_s = _strict_lower(PAGES_PER_STEP * PAGE)
    bands = _pool_bands()

    xp = x_prompt.reshape(rows_p, d)
    xs = x_sample.reshape(rows_s, d)
    nk_s, nv_s, npool_p, npool_s = [], [], [], []
    prev_kv = []
    for i in range(depth):
        mod_p = mods[i, :batch].reshape(batch, 1, 3 * d)
        mod_s = jnp.repeat(mods[i, batch:n_seq], dec_seq, axis=0)
        g_i = norm_g[i].reshape(1, d)
        j = i // 2
        if i % 2 == 0:
            w_out = w_out_a
            gq = q_norm_g[j].reshape(1, HEAD_DIM)
            gk = k_norm_g[j].reshape(1, HEAD_DIM)
            q_s, kf_s, _, vf_s, _, gate_s, *w4 = _proj_a(
                xs, mod_s, None, g_i, w_in_a, j, gq, gk, F32, tm=rows_s)
            q_p, kf_p, kb_p, vf_p, vb_p, gate_p = _proj_a(
                xp, mod_p, seq, g_i, w4, None, gq, gk, BF16, tm=512 if prev_kv else 1024,
                prev_kv=prev_kv)
            prev_kv = [a for layer in range(j + 1) for a in (kf_p[layer], vf_p[layer])]
            bias_rows = jnp.repeat(sb_bias[j], dec_seq).reshape(n_heads * dec_seq, 1)
            o_p, o_s = _attention(q_p, kb_p, vb_p, gate_p, sb_bias[j], tri_p, batch, seq,
                                  page_table, q_s, kf_s[0], vf_s[0], gate_s, bias_rows, tri_s,
                                  cache_k, cache_v, j)
            nk_s.append(kf_s.reshape(dec_batch, dec_seq, n_heads, HEAD_DIM))
            nv_s.append(vf_s.reshape(dec_batch, dec_seq, n_heads, HEAD_DIM))
        else:
            width_b = w_in_b.shape[2] // 2
            w_out = w_out_b
            w_grp = w_grp_b[j].astype(BF16)
            scale = scale_b[j].reshape(1, width_b)
            u_s, gate_s, *w2 = _proj_b(xs, mod_s, None, g_i, w_in_b, j, tm=rows_s)
            u_p, gate_p = _proj_b(xp, mod_p, seq, g_i, w2, None, tm=1024)
            o_p = _pool_prompt(u_p, gate_p, w_grp, scale, bands, seq)
            ext = jnp.concatenate([state_pool[j], u_s.reshape(dec_batch, dec_seq, width_b)], axis=1)
            front = (-POOL_HIST) % 8
            ext_pad = jnp.pad(ext, ((0, 0), (front, 0), (0, 0)))
            o_s = _pool_sample(ext_pad, gate_s, w_grp, scale)
            npool_p.append(u_p.reshape(batch, seq, width_b)[:, seq - POOL_HIST:])
            npool_s.append(ext[:, dec_seq:])
        xs, w_out_bf = _out_proj(o_s, w_out, j, xs, mod_s, None, tm=rows_s, tn=d // 2)
        xp = _out_proj(o_p, [w_out_bf], None, xp, mod_p, seq, tm=512, tn=d)
    n_a = len(prev_kv) // 2
    nk_p = kf_p.reshape(n_a, batch, seq, n_heads, HEAD_DIM)
    nv_p = vf_p.reshape(n_a, batch, seq, n_heads, HEAD_DIM)
    return (xp.reshape(batch, seq, d), xs.reshape(dec_batch, dec_seq, d),
            nk_p, nv_p, jnp.stack(nk_s), jnp.stack(nv_s),
            jnp.stack(npool_p), jnp.stack(npool_s))
```

```python
import functools

import jax
import jax.numpy as jnp
from jax import lax
from jax.experimental import pallas as pl
from jax.experimental.pallas import tpu as pltpu

F32 = jnp.float32
BF16 = jnp.bfloat16

RMS_EPS = 1e-6
HEAD_DIM = 128
PAGE = 128
POOL_WINDOWS = (2, 4, 8, 16)
POOL_HIST = max(POOL_WINDOWS) - 1

VMEM_LIMIT = 52 * 1024 * 1024

Q_TILE = 256
K_TILE = 256
HEADS_PER_STEP = 8
LOG2E = 1.4426950408889634
SOFTPLUS_CLAMP = 64.0
MASKED_LOGIT = 1e30
PAGES_PER_STEP = 4
HEAD_GROUP = 8
POOL_SUB = 128


def _silu(x):
    return x * (1.0 / (1.0 + jnp.exp(-x)))


def _split_bf16(x):
    hi = x.astype(BF16)
    lo = (x - hi.astype(F32)).astype(BF16)
    return hi, lo


def _params(*sem):
    return pltpu.CompilerParams(dimension_semantics=sem, vmem_limit_bytes=VMEM_LIMIT)


def _mod_kernel(c_ref, w_ref, b_ref, o_ref):
    s = _silu(c_ref[...]).astype(BF16)
    w = w_ref[...].astype(BF16)
    o_ref[...] = jnp.dot(s, w, preferred_element_type=F32) + b_ref[...]


def _modulation(c_all, w_mod, b_mod):
    depth, d, n = w_mod.shape
    rows = c_all.shape[0]
    tn = 1024
    return pl.pallas_call(
        _mod_kernel,
        grid=(depth, n // tn),
        in_specs=[
            pl.BlockSpec((rows, d), lambda l, j: (0, 0)),
            pl.BlockSpec((None, d, tn), lambda l, j: (l, 0, j)),
            pl.BlockSpec((None, 1, tn), lambda l, j: (l, 0, j)),
        ],
        out_specs=pl.BlockSpec((None, rows, tn), lambda l, j: (l, 0, j)),
        out_shape=jax.ShapeDtypeStruct((depth, rows, n), F32),
        compiler_params=_params("parallel", "parallel"),
        name="modulation",
    )(c_all, w_mod, b_mod.reshape(depth, 1, n))


def _modulated_norm(x_ref, shift_ref, scale_ref, g_ref, h_ref):
    x = x_ref[...]
    y = x * lax.rsqrt(jnp.mean(x * x, axis=-1, keepdims=True) + RMS_EPS)
    h = (y * g_ref[...]) * (1.0 + scale_ref[...]) + shift_ref[...]
    h_ref[...] = h.astype(BF16)


def _head_rmsnorm(a, g):
    return a * lax.rsqrt(jnp.mean(a * a, axis=-1, keepdims=True) + RMS_EPS) * g


def _mxu_weight(w_ref, copy_ref):
    w = w_ref[...].astype(BF16)
    if copy_ref is not None:
        copy_ref[...] = w
    return w


def _proj_a_kernel(x_ref, shift_ref, scale_ref, g_ref, wq_ref, wk_ref, wv_ref, wg_ref,
                   gq_ref, gk_ref, *rest, q_scale, n_aliased, emit_weights):
    q_ref, kf_ref, kb_ref, vf_ref, vb_ref, gate_ref = rest[n_aliased:n_aliased + 6]
    copies = rest[n_aliased + 6:-1] if emit_weights else (None,) * 4
    h_ref = rest[-1]

    @pl.when(pl.program_id(1) == 0)
    def _():
        _modulated_norm(x_ref, shift_ref, scale_ref, g_ref, h_ref)

    h = h_ref[...]
    tm = q_ref.shape[0]
    n_heads = q_ref.shape[1] // HEAD_DIM
    head_major = len(kf_ref.shape) == 3
    if head_major:
        all_heads = kf_ref.shape[1]
        head0 = pl.program_id(1) * n_heads
        k_rows = kf_ref.reshape(tm * all_heads, HEAD_DIM)
        v_rows = vf_ref.reshape(tm * all_heads, HEAD_DIM)
    q = jnp.dot(h, _mxu_weight(wq_ref, copies[0]), preferred_element_type=F32)
    k = jnp.dot(h, _mxu_weight(wk_ref, copies[1]), preferred_element_type=F32)
    v = jnp.dot(h, _mxu_weight(wv_ref, copies[2]), preferred_element_type=F32)
    for hd in range(n_heads):
        sl = slice(hd * HEAD_DIM, (hd + 1) * HEAD_DIM)
        qn = _head_rmsnorm(q[:, sl], gq_ref[...])
        q_ref[:, sl] = (qn * q_scale).astype(q_ref.dtype)
        kn = _head_rmsnorm(k[:, sl], gk_ref[...])
        kb_ref[:, sl] = kn.astype(kb_ref.dtype)
        if head_major:
            k_rows[pl.ds(head0 + hd, tm, stride=all_heads), :] = kn
            v_rows[pl.ds(head0 + hd, tm, stride=all_heads), :] = v[:, sl]
        else:
            kf_ref[:, sl] = kn
            vf_ref[:, sl] = v[:, sl]
    vb_ref[...] = v.astype(vb_ref.dtype)
    gate_ref[...] = jnp.dot(h, _mxu_weight(wg_ref, copies[3]), preferred_element_type=F32)


def _mod_specs(mod, tm, d, rows_per_seq):
    if rows_per_seq is None:
        return [pl.BlockSpec((tm, d), functools.partial(lambda i, j, c: (i, c), c=c))
                for c in range(3)]
    tiles = rows_per_seq // tm
    return [pl.BlockSpec((None, 1, d),
                         functools.partial(lambda i, j, c: (i // tiles, 0, c), c=c))
            for c in range(3)]


def _section_specs(w_in, layer, n_sections, tn):
    d = w_in.shape[1]
    per_section = w_in.shape[2] // n_sections // tn
    return [pl.BlockSpec((None, d, tn),
                         functools.partial(lambda i, j, s: (layer, 0, s * per_section + j), s=s))
            for s in range(n_sections)]


def _weight_operands(w, layer, n_sections, tn):
    if layer is None:
        d, width = w[0].shape
        return list(w), [pl.BlockSpec((d, tn), lambda i, j: (0, j))] * n_sections, [], []
    d, width = w.shape[1], w.shape[2] // n_sections
    copy_spec = pl.BlockSpec((d, tn), lambda i, j: (0, j))
    return ([w] * n_sections, _section_specs(w, layer, n_sections, tn),
            [copy_spec] * n_sections, [jax.ShapeDtypeStruct((d, width), BF16)] * n_sections)


def _proj_a(x, mod, rows_per_seq, norm_g, w, layer, gq, gk, small_dtype, tm, tn=256,
            kv_slab=None, kv_stack=None):
    rows, d = x.shape
    w_args, w_specs, copy_specs, copy_shapes = _weight_operands(w, layer, 4, tn)
    width = w_args[0].shape[-1] // (1 if layer is None else 4)
    shift_spec, scale_spec, _ = _mod_specs(mod, tm, d, rows_per_seq)
    o_spec = pl.BlockSpec((tm, tn), lambda i, j: (i, j))
    row_spec = pl.BlockSpec((1, HEAD_DIM), lambda i, j: (0, 0))
    f32_out = jax.ShapeDtypeStruct((rows, width), F32)
    small_out = jax.ShapeDtypeStruct((rows, width), small_dtype)
    x_spec = pl.BlockSpec((tm, d), lambda i, j: (i, 0))
    if kv_slab is None:
        kv_spec, kv_out = o_spec, f32_out
    else:
        slab, count = kv_slab
        n_heads = width // HEAD_DIM
        kv_spec = pl.BlockSpec((None, tm, n_heads, HEAD_DIM), lambda i, j: (slab, i, 0, 0))
        kv_out = jax.ShapeDtypeStruct((count, rows, n_heads, HEAD_DIM), F32)
    aliased = list(kv_stack or ())
    n_in = 6 + len(w_args)
    return pl.pallas_call(
        functools.partial(_proj_a_kernel, q_scale=HEAD_DIM ** -0.5 * LOG2E,
                          n_aliased=len(aliased), emit_weights=layer is not None),
        grid=(rows // tm, width // tn),
        in_specs=[x_spec, shift_spec, scale_spec,
                  pl.BlockSpec((1, d), lambda i, j: (0, 0)),
                  *w_specs, row_spec, row_spec,
                  *([pl.BlockSpec(memory_space=pl.ANY)] * len(aliased))],
        out_specs=[o_spec, kv_spec, o_spec, kv_spec, o_spec, o_spec, *copy_specs],
        out_shape=[small_out, kv_out, small_out, kv_out, small_out, f32_out, *copy_shapes],
        input_output_aliases={n_in + a: out for a, out in zip(range(len(aliased)), (1, 3))},
        scratch_shapes=[pltpu.VMEM((tm, d), BF16)],
        compiler_params=_params("parallel", "arbitrary"),
        name="proj_a",
    )(x, mod, mod, norm_g, *w_args, gq, gk, *aliased)


def _proj_b_kernel(x_ref, shift_ref, scale_ref, g_ref, wu_ref, wg_ref, u_ref, gate_ref, *rest,
                   emit_weights):
    copies = rest[:-1] if emit_weights else (None,) * 2
    h_ref = rest[-1]

    @pl.when(pl.program_id(1) == 0)
    def _():
        _modulated_norm(x_ref, shift_ref, scale_ref, g_ref, h_ref)

    h = h_ref[...]
    u_ref[...] = jnp.dot(h, _mxu_weight(wu_ref, copies[0]), preferred_element_type=F32)
    gate_ref[...] = jnp.dot(h, _mxu_weight(wg_ref, copies[1]), preferred_element_type=F32)


def _proj_b(x, mod, rows_per_seq, norm_g, w, layer, tm, tn=512):
    rows, d = x.shape
    w_args, w_specs, copy_specs, copy_shapes = _weight_operands(w, layer, 2, tn)
    width = w_args[0].shape[-1] // (1 if layer is None else 2)
    shift_spec, scale_spec, _ = _mod_specs(mod, tm, d, rows_per_seq)
    o_spec = pl.BlockSpec((tm, tn), lambda i, j: (i, j))
    f32_out = jax.ShapeDtypeStruct((rows, width), F32)
    return pl.pallas_call(
        functools.partial(_proj_b_kernel, emit_weights=layer is not None),
        grid=(rows // tm, width // tn),
        in_specs=[pl.BlockSpec((tm, d), lambda i, j: (i, 0)), shift_spec, scale_spec,
                  pl.BlockSpec((1, d), lambda i, j: (0, 0)), *w_specs],
        out_specs=[o_spec, o_spec, *copy_specs],
        out_shape=[f32_out, f32_out, *copy_shapes],
        scratch_shapes=[pltpu.VMEM((tm, d), BF16)],
        compiler_params=_params("parallel", "arbitrary"),
        name="proj_b",
    )(x, mod, mod, norm_g, *w_args)


def _out_kernel(inp_ref, w_ref, x_ref, gmod_ref, o_ref, *copy):
    w = _mxu_weight(w_ref, copy[0] if copy else None)
    y = jnp.dot(inp_ref[...].astype(BF16), w, preferred_element_type=F32)
    o_ref[...] = x_ref[...] + gmod_ref[...] * y


def _out_proj(inp, w, layer, x, mod, rows_per_seq, tm, tn):
    rows, d = x.shape
    kdim = inp.shape[1]
    w_args, w_specs, copy_specs, copy_shapes = _weight_operands(w, layer, 1, tn)
    if rows_per_seq is None:
        gmod_spec = pl.BlockSpec((tm, tn), lambda i, j: (i, 2 * (d // tn) + j))
    else:
        tiles = rows_per_seq // tm
        gmod_spec = pl.BlockSpec((None, 1, tn), lambda i, j: (i // tiles, 0, 2 * (d // tn) + j))
    o_spec = pl.BlockSpec((tm, tn), lambda i, j: (i, j))
    outs = pl.pallas_call(
        _out_kernel,
        grid=(rows // tm, d // tn),
        in_specs=[pl.BlockSpec((tm, kdim), lambda i, j: (i, 0)), *w_specs, o_spec, gmod_spec],
        out_specs=[o_spec, *copy_specs],
        out_shape=[jax.ShapeDtypeStruct((rows, d), F32), *copy_shapes],
        compiler_params=_params("parallel", "parallel"),
        name="out_proj",
    )(inp, *w_args, x, mod)
    return outs if copy_shapes else outs[0]


def _sb_block(z, tri, mask, carry):
    log_beta, l, tot = _sb_survival(z, mask)
    later = jnp.dot(l, tri, preferred_element_type=F32)
    return _sb_weights(log_beta, later, carry, mask), tot


def _sb_survival(z, mask):
    sp = jnp.maximum(jnp.log2(1.0 + jnp.exp2(jnp.minimum(z, SOFTPLUS_CLAMP))), z)
    l = sp if mask is None else jnp.where(mask, sp, 0.0)
    return z - sp, l.astype(BF16), jnp.sum(l, axis=1, keepdims=True)


def _sb_weights(log_beta, later, carry, mask):
    logp = log_beta - later
    p = jnp.exp2(logp if carry is None else logp - carry)
    return p if mask is None else jnp.where(mask, p, 0.0)


def _attn_kernel(pt_ref, qi_ref, kj_ref,
                 bias_ref, q_ref, k_ref, v_ref, gate_ref, tri_ref,
                 qs_ref, kn_ref, vn_ref, gs_ref, bias_rows_ref, tri_s_ref, *rest,
                 n_heads, steps_per_group, sample_steps):
    del pt_ref
    n_parts = PAGES_PER_STEP * (n_heads // HEAD_GROUP)
    k_pages, v_pages = rest[:n_parts], rest[n_parts:2 * n_parts]
    (o_ref, os_ref, acc_ref, carry_ref,
     qbd_ref, knew_ref, vnew_ref, accs_ref, carrys_ref) = rest[2 * n_parts:]
    step = pl.program_id(0)
    t = step % steps_per_group
    head0 = (step // steps_per_group) % (n_heads // HEADS_PER_STEP) * HEADS_PER_STEP
    qi, kj = qi_ref[step], kj_ref[step]
    n_tok = qs_ref.shape[0]
    n_chunks = n_heads // 2
    cw = 2 * HEAD_DIM
    rows = n_heads * n_tok
    heads = [slice(hh * HEAD_DIM, (hh + 1) * HEAD_DIM) for hh in range(HEADS_PER_STEP)]

    def new_chunk(ref, c):
        return ref[:, c * cw:(c + 1) * cw].astype(BF16)

    def page_chunk(parts, c):
        n_groups = n_heads // HEAD_GROUP
        grp, local = divmod(2 * c, HEAD_GROUP)

        def head(r, hd):
            flat = r.reshape(PAGE * HEAD_GROUP, HEAD_DIM)
            return flat[pl.ds(hd, PAGE, stride=HEAD_GROUP), :]
        return jnp.concatenate(
            [jnp.concatenate([head(r, local), head(r, local + 1)], axis=1)
             for r in parts[grp::n_groups]], axis=0).astype(BF16)

    def sample_scores(load_k, c):
        return lax.dot_general(qbd_ref[c], load_k(c), (((1,), (1,)), ((), ())),
                               preferred_element_type=F32)

    def sample_weights(z_chunks, width, mask):
        z = jnp.concatenate(z_chunks, axis=0) + bias_rows_ref[...] * LOG2E
        carry = carrys_ref[...]
        p, tot = _sb_block(z, tri_s_ref[:width, :width], mask, carry[:, :1])
        carrys_ref[...] = carry + tot
        return p.astype(BF16)

    def sample_values(p, load_v, c):
        rs = slice(c * 2 * n_tok, (c + 1) * 2 * n_tok)
        accs_ref[rs, :] += jnp.dot(p[rs], load_v(c), preferred_element_type=F32)

    @pl.when(t == 0)
    def _():
        accs_ref[...] = jnp.zeros_like(accs_ref)
        carrys_ref[...] = jnp.zeros_like(carrys_ref)
        zero = jnp.zeros((n_tok, HEAD_DIM), F32)
        for c in range(n_chunks):
            q0 = qs_ref[:, (2 * c) * HEAD_DIM:(2 * c + 1) * HEAD_DIM]
            q1 = qs_ref[:, (2 * c + 1) * HEAD_DIM:(2 * c + 2) * HEAD_DIM]
            top = jnp.concatenate([q0, zero], axis=1)
            bot = jnp.concatenate([zero, q1], axis=1)
            qbd_ref[c] = jnp.concatenate([top, bot], axis=0).astype(BF16)
        knew_ref[...] = jnp.zeros_like(knew_ref)
        vnew_ref[...] = jnp.zeros_like(vnew_ref)
        knew_ref[:n_tok, :] = kn_ref[...]
        vnew_ref[:n_tok, :] = vn_ref[...]
        tok = lax.broadcasted_iota(jnp.int32, (rows, PAGE), 0) % n_tok
        col = lax.broadcasted_iota(jnp.int32, (rows, PAGE), 1)
        load_k = functools.partial(new_chunk, knew_ref)
        load_v = functools.partial(new_chunk, vnew_ref)
        p = sample_weights([sample_scores(load_k, c) for c in range(n_chunks)], PAGE, col < tok)
        for c in range(n_chunks):
            sample_values(p, load_v, c)

    @pl.when(kj == qi)
    def _():
        acc_ref[...] = jnp.zeros_like(acc_ref)
        carry_ref[...] = jnp.zeros_like(carry_ref)

    def block(with_sample, diagonal):
        hs = range(HEADS_PER_STEP)
        tri = tri_ref[...]
        if diagonal:
            row = lax.broadcasted_iota(jnp.int32, (Q_TILE, K_TILE), 0)
            col = lax.broadcasted_iota(jnp.int32, (Q_TILE, K_TILE), 1)
            keep = col < row
        st = [dict() for _ in hs]
        load_k = functools.partial(page_chunk, k_pages)
        load_v = functools.partial(page_chunk, v_pages)
        zs = []

        def qk(hh):
            z = lax.dot_general(q_ref[:, heads[hh]], k_ref[:, heads[hh]], (((1,), (1,)), ((), ())),
                                preferred_element_type=F32)
            z = z + bias_ref[head0 + hh] * LOG2E
            st[hh]["z"] = jnp.where(keep, z, -MASKED_LOGIT) if diagonal else z
            if with_sample:
                zs.append(sample_scores(load_k, hh))

        def survive(hh):
            s = st[hh]
            s["log_beta"], l, s["tot"] = _sb_survival(s.pop("z"), None)
            s["later"] = jnp.dot(l, tri, preferred_element_type=F32)

        def weigh(hh):
            s = st[hh]
            s["carry"] = carry_ref[hh]
            carry = jnp.concatenate([s["carry"]] * (K_TILE // HEAD_DIM), axis=1)
            p = _sb_weights(s.pop("log_beta"), s.pop("later"), carry, None)
            s["pv"] = jnp.dot(p.astype(BF16), v_ref[:, heads[hh]], preferred_element_type=F32)
            if with_sample:
                sample_values(st[0]["ps"], load_v, hh)

        def accumulate(hh):
            s = st[hh]
            acc_ref[hh] += s["pv"]
            carry_ref[hh] = s["carry"] + s["tot"]

        for hh in hs:
            qk(hh)
        for hh in hs:
            survive(hh)
        if with_sample:
            st[0]["ps"] = sample_weights(zs, PAGES_PER_STEP * PAGE, None)
        for hh in hs:
            weigh(hh)
        for hh in hs:
            accumulate(hh)

    for with_sample in (True, False):
        for diagonal in (True, False):
            pl.when(((t < sample_steps) == with_sample) & ((kj == qi) == diagonal))(
                functools.partial(block, with_sample, diagonal))

    @pl.when(kj == 0)
    def _():
        for hh in range(HEADS_PER_STEP):
            o_ref[:, heads[hh]] = (acc_ref[hh] * _silu(gate_ref[:, heads[hh]])).astype(o_ref.dtype)

    @pl.when(t == sample_steps - 1)
    def _():
        for hd in range(n_heads):
            c, hh = hd // 2, hd % 2
            r0 = (c * 2 + hh) * n_tok
            o = accs_ref[r0:r0 + n_tok, hh * HEAD_DIM:(hh + 1) * HEAD_DIM]
            sl = slice(hd * HEAD_DIM, (hd + 1) * HEAD_DIM)
            os_ref[:, sl] = o * _silu(gs_ref[:, sl])


def _attention(q, k, v, gate, bias, tri, batch, seq,
               page_table, q_s, k_new, v_new, gate_s, bias_rows, tri_s, cache_k, cache_v, layer):
    rows, width = q.shape
    n_heads = width // HEAD_DIM
    nq = seq // Q_TILE
    head_groups = n_heads // HEADS_PER_STEP
    dec_batch, n_pages = page_table.shape
    rows_s = q_s.shape[0]
    n_tok = rows_s // dec_batch
    npg = PAGES_PER_STEP
    sample_steps = n_pages // npg
    walk = [(i, j) for i in range(nq) for j in range(i, -1, -1)]
    steps_per_group = len(walk)
    n_groups = batch * head_groups
    assert n_groups == dec_batch and sample_steps <= steps_per_group
    assert K_TILE == Q_TILE and HEADS_PER_STEP * 2 == n_heads
    qi_tab = jnp.asarray([w[0] for w in walk] * n_groups, jnp.int32)
    kj_tab = jnp.asarray([w[1] for w in walk] * n_groups, jnp.int32)
    hw = HEADS_PER_STEP * HEAD_DIM

    def group(s):
        g = s // steps_per_group
        return g // head_groups, g % head_groups

    def q_index(s, pt, qi, kj):
        b, hg = group(s)
        return (b * nq + qi[s], hg)

    def k_index(s, pt, qi, kj):
        b, hg = group(s)
        return (b * nq + kj[s], hg)

    q_spec = pl.BlockSpec((Q_TILE, hw), q_index)
    k_spec = pl.BlockSpec((K_TILE, hw), k_index)
    tok_spec = pl.BlockSpec((n_tok, width), lambda s, pt, qi, kj: (s // steps_per_group, 0))

    def page_spec(p, grp):
        def index(s, pt, qi, kj):
            g = jnp.minimum(s % steps_per_group, sample_steps - 1)
            return (layer, pt[s // steps_per_group, (sample_steps - 1 - g) * npg + p], 0, grp, 0)
        return pl.BlockSpec((None, None, PAGE, HEAD_GROUP, HEAD_DIM), index)

    page_specs = [page_spec(p, grp) for p in range(npg) for grp in range(n_heads // HEAD_GROUP)]
    const = lambda s, pt, qi, kj: (0, 0)
    acc = pltpu.VMEM((HEADS_PER_STEP, Q_TILE, HEAD_DIM), F32)
    grid_spec = pltpu.PrefetchScalarGridSpec(
        num_scalar_prefetch=3,
        grid=(n_groups * steps_per_group,),
        in_specs=[pl.BlockSpec(memory_space=pltpu.SMEM), q_spec, k_spec, k_spec, q_spec,
                  pl.BlockSpec((K_TILE, K_TILE), const),
                  tok_spec, tok_spec, tok_spec, tok_spec,
                  pl.BlockSpec((n_heads * n_tok, 1), const),
                  pl.BlockSpec((npg * PAGE, npg * PAGE), const)]
                 + page_specs + page_specs,
        out_specs=[q_spec, tok_spec],
        scratch_shapes=[acc, acc,
                        pltpu.VMEM((n_heads // 2, 2 * n_tok, 2 * HEAD_DIM), BF16),
                        pltpu.VMEM((PAGE, width), F32),
                        pltpu.VMEM((PAGE, width), F32),
                        pltpu.VMEM((n_heads * n_tok, 2 * HEAD_DIM), F32),
                        pltpu.VMEM((n_heads * n_tok, HEAD_DIM), F32)],
    )
    return pl.pallas_call(
        functools.partial(_attn_kernel, n_heads=n_heads, steps_per_group=steps_per_group,
                          sample_steps=sample_steps),
        grid_spec=grid_spec,
        out_shape=[jax.ShapeDtypeStruct((rows, width), BF16),
                   jax.ShapeDtypeStruct((rows_s, width), F32)],
        compiler_params=_params("arbitrary"),
        name="attention",
    )(page_table, qi_tab, kj_tab, bias, q, k, v, gate, tri,
      q_s, k_new, v_new, gate_s, bias_rows, tri_s,
      *([cache_k] * len(page_specs)), *([cache_v] * len(page_specs)))


def _group_mix(mean, u, gate, w_grp, scale):
    diff = (mean - u).astype(BF16)
    return jnp.dot(diff, w_grp, preferred_element_type=F32) * scale * _silu(gate)


def _pool_prompt_kernel(u_ref, halo_ref, gate_ref, wgrp_ref, scale_ref, band_ref, o_ref,
                        hi_ref, lo_ref, *, tiles_per_seq):
    tm = u_ref.shape[0]
    gdim = wgrp_ref.shape[1]
    tile = pl.program_id(0) % tiles_per_seq
    halo = jnp.where(tile == 0, 0.0, halo_ref[...])
    hi, lo = _split_bf16(halo)
    hi_ref[:POOL_SUB, :] = hi
    lo_ref[:POOL_SUB, :] = lo
    hi, lo = _split_bf16(u_ref[...])
    hi_ref[POOL_SUB:, :] = hi
    lo_ref[POOL_SUB:, :] = lo
    for m in range(tm // POOL_SUB):
        r0 = m * POOL_SUB
        pos = tile * tm + r0 + lax.broadcasted_iota(jnp.int32, (POOL_SUB, 1), 0)
        for g, w in enumerate(POOL_WINDOWS):
            cs = slice(g * gdim, (g + 1) * gdim)
            band = band_ref[g]
            total = (jnp.dot(band, hi_ref[r0:r0 + 2 * POOL_SUB, cs], preferred_element_type=F32)
                     + jnp.dot(band, lo_ref[r0:r0 + 2 * POOL_SUB, cs], preferred_element_type=F32))
            inv_cnt = 1.0 / jnp.minimum(w, pos + 1).astype(F32)
            mixed = _group_mix(total * inv_cnt, u_ref[r0:r0 + POOL_SUB, cs],
                               gate_ref[r0:r0 + POOL_SUB, cs], wgrp_ref[g], scale_ref[:, cs])
            o_ref[r0:r0 + POOL_SUB, cs] = mixed.astype(o_ref.dtype)


def _pool_prompt(u, gate, w_grp, scale, bands, seq, tm=512):
    rows, width = u.shape
    sub_per_tile = tm // POOL_SUB
    tile_spec = pl.BlockSpec((tm, width), lambda i: (i, 0))
    return pl.pallas_call(
        functools.partial(_pool_prompt_kernel, tiles_per_seq=seq // tm),
        grid=(rows // tm,),
        in_specs=[tile_spec,
                  pl.BlockSpec((POOL_SUB, width), lambda i: (jnp.maximum(i * sub_per_tile - 1, 0), 0)),
                  tile_spec,
                  pl.BlockSpec(w_grp.shape, lambda i: (0, 0, 0)),
                  pl.BlockSpec((1, width), lambda i: (0, 0)),
                  pl.BlockSpec(bands.shape, lambda i: (0, 0, 0))],
        out_specs=tile_spec,
        out_shape=jax.ShapeDtypeStruct((rows, width), BF16),
        scratch_shapes=[pltpu.VMEM((tm + POOL_SUB, width), BF16),
                        pltpu.VMEM((tm + POOL_SUB, width), BF16)],
        compiler_params=_params("parallel"),
        name="pool_prompt",
    )(u, u, gate, w_grp, scale, bands)


def _pool_sample_kernel(ext_ref, gate_ref, wgrp_ref, scale_ref, o_ref):
    n_tok = gate_ref.shape[0]
    gdim = wgrp_ref.shape[1]
    last = ext_ref.shape[0] - n_tok
    for g, w in enumerate(POOL_WINDOWS):
        cs = slice(g * gdim, (g + 1) * gdim)
        u = ext_ref[last:last + n_tok, cs]
        total = u
        for back in range(1, w):
            total = total + ext_ref[last - back:last - back + n_tok, cs]
        o_ref[:, cs] = _group_mix(total / float(w), u, gate_ref[:, cs], wgrp_ref[g], scale_ref[:, cs])


def _pool_sample(ext, gate, w_grp, scale):
    batch, ext_rows, width = ext.shape
    rows = gate.shape[0]
    n_tok = rows // batch
    tok_spec = pl.BlockSpec((n_tok, width), lambda b: (b, 0))
    return pl.pallas_call(
        _pool_sample_kernel,
        grid=(batch,),
        in_specs=[pl.BlockSpec((None, ext_rows, width), lambda b: (b, 0, 0)),
                  tok_spec,
                  pl.BlockSpec(w_grp.shape, lambda b: (0, 0, 0)),
                  pl.BlockSpec((1, width), lambda b: (0, 0))],
        out_specs=tok_spec,
        out_shape=jax.ShapeDtypeStruct((rows, width), F32),
        compiler_params=_params("parallel"),
        name="pool_sample",
    )(ext, gate, w_grp, scale)


def _strict_lower(n):
    j = lax.broadcasted_iota(jnp.int32, (n, n), 0)
    s = lax.broadcasted_iota(jnp.int32, (n, n), 1)
    return (j > s).astype(BF16)


def _pool_bands():
    r = lax.broadcasted_iota(jnp.int32, (POOL_SUB, 2 * POOL_SUB), 0)
    c = lax.broadcasted_iota(jnp.int32, (POOL_SUB, 2 * POOL_SUB), 1)
    back = POOL_SUB + r - c
    return jnp.stack([((back >= 0) & (back < w)).astype(BF16) for w in POOL_WINDOWS])


def kernel(x_prompt, x_sample, cache_k, cache_v, state_pool, page_table, c_prompt, c_sample,
           norm_g, w_mod, b_mod, w_in_a, q_norm_g, k_norm_g, sb_bias, w_out_a,
           w_in_b, w_grp_b, scale_b, w_out_b):
    batch, seq, d = x_prompt.shape
    dec_batch, dec_seq, _ = x_sample.shape
    depth = norm_g.shape[0]
    n_heads = sb_bias.shape[1]
    width_a = n_heads * HEAD_DIM
    n_phys = cache_k.shape[1]
    rows_p, rows_s = batch * seq, dec_batch * dec_seq
    assert cache_k.shape[2] == PAGE and cache_k.shape[4] == HEAD_DIM
    assert state_pool.shape[2] == POOL_HIST and seq % 512 == 0

    n_seq = batch + dec_batch
    pad = (-n_seq) % 8
    c_all = jnp.concatenate([c_prompt, c_sample, jnp.zeros((pad, d), F32)], axis=0)
    mods = _modulation(c_all, w_mod, b_mod)

    tri_p = _strict_lower(K_TILE)
    tri_s = _strict_lower(PAGES_PER_STEP * PAGE)
    bands = _pool_bands()

    xp = x_prompt.reshape(rows_p, d)
    xs = x_sample.reshape(rows_s, d)
    nk_s, nv_s, npool_p, npool_s = [], [], [], []
    n_a = (depth + 1) // 2
    kv_stack = tuple(jnp.zeros((n_a, rows_p, n_heads, HEAD_DIM), F32) for _ in range(2))
    for i in range(depth):
        mod_p = mods[i, :batch].reshape(batch, 1, 3 * d)
        mod_s = jnp.repeat(mods[i, batch:n_seq], dec_seq, axis=0)
        g_i = norm_g[i].reshape(1, d)
        j = i // 2
        if i % 2 == 0:
            w_out = w_out_a
            gq = q_norm_g[j].reshape(1, HEAD_DIM)
            gk = k_norm_g[j].reshape(1, HEAD_DIM)
            q_s, kf_s, _, vf_s, _, gate_s, *w4 = _proj_a(
                xs, mod_s, None, g_i, w_in_a, j, gq, gk, F32, tm=rows_s)
            q_p, k_all, kb_p, v_all, vb_p, gate_p = _proj_a(
                xp, mod_p, seq, g_i, w4, None, gq, gk, BF16, tm=512,
                kv_slab=(j, n_a), kv_stack=kv_stack)
            kv_stack = (k_all, v_all)
            bias_rows = jnp.repeat(sb_bias[j], dec_seq).reshape(n_heads * dec_seq, 1)
            o_p, o_s = _attention(q_p, kb_p, vb_p, gate_p, sb_bias[j], tri_p, batch, seq,
                                  page_table, q_s, kf_s, vf_s, gate_s, bias_rows, tri_s,
                                  cache_k, cache_v, j)
            nk_s.append(kf_s.reshape(dec_batch, dec_seq, n_heads, HEAD_DIM))
            nv_s.append(vf_s.reshape(dec_batch, dec_seq, n_heads, HEAD_DIM))
        else:
            width_b = w_in_b.shape[2] // 2
            w_out = w_out_b
            w_grp = w_grp_b[j].astype(BF16)
            scale = scale_b[j].reshape(1, width_b)
            u_s, gate_s, *w2 = _proj_b(xs, mod_s, None, g_i, w_in_b, j, tm=rows_s)
            u_p, gate_p = _proj_b(xp, mod_p, seq, g_i, w2, None, tm=1024)
            o_p = _pool_prompt(u_p, gate_p, w_grp, scale, bands, seq)
            ext = jnp.concatenate([state_pool[j], u_s.reshape(dec_batch, dec_seq, width_b)], axis=1)
            front = (-POOL_HIST) % 8
            ext_pad = jnp.pad(ext, ((0, 0), (front, 0), (0, 0)))
            o_s = _pool_sample(ext_pad, gate_s, w_grp, scale)
            npool_p.append(u_p.reshape(batch, seq, width_b)[:, seq - POOL_HIST:])
            npool_s.append(ext[:, dec_seq:])
        xs, w_out_bf = _out_proj(o_s, w_out, j, xs, mod_s, None, tm=rows_s, tn=d // 2)
        xp = _out_proj(o_p, [w_out_bf], None, xp, mod_p, seq, tm=512, tn=d)
    nk_p = kv_stack[0].reshape(n_a, batch, seq, n_heads, HEAD_DIM)
    nv_p = kv_stack[1].reshape(n_a, batch, seq, n_heads, HEAD_DIM)
    return (xp.reshape(batch, seq, d), xs.reshape(dec_batch, dec_seq, d),
            nk_p, nv_p, jnp.stack(nk_s), jnp.stack(nv_s),
            jnp.stack(npool_p), jnp.stack(npool_s))
```

```python
import functools

import jax
import jax.numpy as jnp
from jax import lax
from jax.experimental import pallas as pl
from jax.experimental.pallas import tpu as pltpu

F32 = jnp.float32
BF16 = jnp.bfloat16

RMS_EPS = 1e-6
HEAD_DIM = 128
PAGE = 128
POOL_WINDOWS = (2, 4, 8, 16)
POOL_HIST = max(POOL_WINDOWS) - 1

VMEM_LIMIT = 52 * 1024 * 1024

Q_TILE = 256
K_TILE = 256
HEADS_PER_STEP = 8
LOG2E = 1.4426950408889634
SOFTPLUS_CLAMP = 64.0
MASKED_LOGIT = 1e30
PAGES_PER_STEP = 4
HEAD_GROUP = 8
POOL_SUB = 128


def _silu(x):
    return x * (1.0 / (1.0 + jnp.exp(-x)))


def _split_bf16(x):
    hi = x.astype(BF16)
    lo = (x - hi.astype(F32)).astype(BF16)
    return hi, lo


def _params(*sem):
    return pltpu.CompilerParams(dimension_semantics=sem, vmem_limit_bytes=VMEM_LIMIT)


def _mod_kernel(c_ref, w_ref, b_ref, o_ref):
    s = _silu(c_ref[...]).astype(BF16)
    w = w_ref[...].astype(BF16)
    o_ref[...] = jnp.dot(s, w, preferred_element_type=F32) + b_ref[...]


def _modulation(c_all, w_mod, b_mod):
    depth, d, n = w_mod.shape
    rows = c_all.shape[0]
    tn = 1024
    return pl.pallas_call(
        _mod_kernel,
        grid=(depth, n // tn),
        in_specs=[
            pl.BlockSpec((rows, d), lambda l, j: (0, 0)),
            pl.BlockSpec((None, d, tn), lambda l, j: (l, 0, j)),
            pl.BlockSpec((None, 1, tn), lambda l, j: (l, 0, j)),
        ],
        out_specs=pl.BlockSpec((None, rows, tn), lambda l, j: (l, 0, j)),
        out_shape=jax.ShapeDtypeStruct((depth, rows, n), F32),
        compiler_params=_params("parallel", "parallel"),
        name="modulation",
    )(c_all, w_mod, b_mod.reshape(depth, 1, n))


def _modulated_norm(x_ref, shift_ref, scale_ref, g_ref, h_ref):
    x = x_ref[...]
    y = x * lax.rsqrt(jnp.mean(x * x, axis=-1, keepdims=True) + RMS_EPS)
    h = (y * g_ref[...]) * (1.0 + scale_ref[...]) + shift_ref[...]
    h_ref[...] = h.astype(BF16)


def _head_rmsnorm(a, g):
    return a * lax.rsqrt(jnp.mean(a * a, axis=-1, keepdims=True) + RMS_EPS) * g


def _mxu_weight(w_ref, copy_ref):
    w = w_ref[...].astype(BF16)
    if copy_ref is not None:
        copy_ref[...] = w
    return w


def _proj_a_kernel(x_ref, shift_ref, scale_ref, g_ref, wq_ref, wk_ref, wv_ref, wg_ref,
                   gq_ref, gk_ref, *rest, q_scale, n_aliased, emit_weights):
    q_ref, kf_ref, kb_ref, vf_ref, vb_ref, gate_ref = rest[n_aliased:n_aliased + 6]
    copies = rest[n_aliased + 6:-1] if emit_weights else (None,) * 4
    h_ref = rest[-1]

    @pl.when(pl.program_id(1) == 0)
    def _():
        _modulated_norm(x_ref, shift_ref, scale_ref, g_ref, h_ref)

    h = h_ref[...]
    tm = q_ref.shape[0]
    n_heads = q_ref.shape[1] // HEAD_DIM
    head_major = len(kf_ref.shape) == 3
    if head_major:
        all_heads = kf_ref.shape[1]
        head0 = pl.program_id(1) * n_heads
        k_rows = kf_ref.reshape(tm * all_heads, HEAD_DIM)
        v_rows = vf_ref.reshape(tm * all_heads, HEAD_DIM)
    q = jnp.dot(h, _mxu_weight(wq_ref, copies[0]), preferred_element_type=F32)
    k = jnp.dot(h, _mxu_weight(wk_ref, copies[1]), preferred_element_type=F32)
    v = jnp.dot(h, _mxu_weight(wv_ref, copies[2]), preferred_element_type=F32)
    for hd in range(n_heads):
        sl = slice(hd * HEAD_DIM, (hd + 1) * HEAD_DIM)
        qn = _head_rmsnorm(q[:, sl], gq_ref[...])
        q_ref[:, sl] = (qn * q_scale).astype(q_ref.dtype)
        kn = _head_rmsnorm(k[:, sl], gk_ref[...])
        kb_ref[:, sl] = kn.astype(kb_ref.dtype)
        if head_major:
            k_rows[pl.ds(head0 + hd, tm, stride=all_heads), :] = kn
            v_rows[pl.ds(head0 + hd, tm, stride=all_heads), :] = v[:, sl]
        else:
            kf_ref[:, sl] = kn
            vf_ref[:, sl] = v[:, sl]
    vb_ref[...] = v.astype(vb_ref.dtype)
    gate_ref[...] = jnp.dot(h, _mxu_weight(wg_ref, copies[3]), preferred_element_type=F32)


def _mod_specs(mod, tm, d, rows_per_seq):
    if rows_per_seq is None:
        return [pl.BlockSpec((tm, d), functools.partial(lambda i, j, c: (i, c), c=c))
                for c in range(3)]
    tiles = rows_per_seq // tm
    return [pl.BlockSpec((None, 1, d),
                         functools.partial(lambda i, j, c: (i // tiles, 0, c), c=c))
            for c in range(3)]


def _section_specs(w_in, layer, n_sections, tn):
    d = w_in.shape[1]
    per_section = w_in.shape[2] // n_sections // tn
    return [pl.BlockSpec((None, d, tn),
                         functools.partial(lambda i, j, s: (layer, 0, s * per_section + j), s=s))
            for s in range(n_sections)]


def _weight_operands(w, layer, n_sections, tn):
    if layer is None:
        d, width = w[0].shape
        return list(w), [pl.BlockSpec((d, tn), lambda i, j: (0, j))] * n_sections, [], []
    d, width = w.shape[1], w.shape[2] // n_sections
    copy_spec = pl.BlockSpec((d, tn), lambda i, j: (0, j))
    return ([w] * n_sections, _section_specs(w, layer, n_sections, tn),
            [copy_spec] * n_sections, [jax.ShapeDtypeStruct((d, width), BF16)] * n_sections)


def _proj_a(x, mod, rows_per_seq, norm_g, w, layer, gq, gk, small_dtype, tm, tn=256,
            kv_slab=None, kv_stack=None):
    rows, d = x.shape
    w_args, w_specs, copy_specs, copy_shapes = _weight_operands(w, layer, 4, tn)
    width = w_args[0].shape[-1] // (1 if layer is None else 4)
    shift_spec, scale_spec, _ = _mod_specs(mod, tm, d, rows_per_seq)
    o_spec = pl.BlockSpec((tm, tn), lambda i, j: (i, j))
    row_spec = pl.BlockSpec((1, HEAD_DIM), lambda i, j: (0, 0))
    f32_out = jax.ShapeDtypeStruct((rows, width), F32)
    small_out = jax.ShapeDtypeStruct((rows, width), small_dtype)
    x_spec = pl.BlockSpec((tm, d), lambda i, j: (i, 0))
    if kv_slab is None:
        kv_spec, kv_out = o_spec, f32_out
    else:
        slab, count = kv_slab
        n_heads = width // HEAD_DIM
        kv_spec = pl.BlockSpec((None, tm, n_heads, HEAD_DIM), lambda i, j: (slab, i, 0, 0))
        kv_out = jax.ShapeDtypeStruct((count, rows, n_heads, HEAD_DIM), F32)
    aliased = list(kv_stack or ())
    n_in = 6 + len(w_args)
    return pl.pallas_call(
        functools.partial(_proj_a_kernel, q_scale=HEAD_DIM ** -0.5 * LOG2E,
                          n_aliased=len(aliased), emit_weights=layer is not None),
        grid=(rows // tm, width // tn),
        in_specs=[x_spec, shift_spec, scale_spec,
                  pl.BlockSpec((1, d), lambda i, j: (0, 0)),
                  *w_specs, row_spec, row_spec,
                  *([pl.BlockSpec(memory_space=pl.ANY)] * len(aliased))],
        out_specs=[o_spec, kv_spec, o_spec, kv_spec, o_spec, o_spec, *copy_specs],
        out_shape=[small_out, kv_out, small_out, kv_out, small_out, f32_out, *copy_shapes],
        input_output_aliases={n_in + a: out for a, out in zip(range(len(aliased)), (1, 3))},
        scratch_shapes=[pltpu.VMEM((tm, d), BF16)],
        compiler_params=_params("parallel", "arbitrary"),
        name="proj_a",
    )(x, mod, mod, norm_g, *w_args, gq, gk, *aliased)


def _proj_b_kernel(x_ref, shift_ref, scale_ref, g_ref, wu_ref, wg_ref, u_ref, gate_ref, *rest,
                   emit_weights):
    copies = rest[:-1] if emit_weights else (None,) * 2
    h_ref = rest[-1]

    @pl.when(pl.program_id(1) == 0)
    def _():
        _modulated_norm(x_ref, shift_ref, scale_ref, g_ref, h_ref)

    h = h_ref[...]
    u_ref[...] = jnp.dot(h, _mxu_weight(wu_ref, copies[0]), preferred_element_type=F32)
    gate_ref[...] = jnp.dot(h, _mxu_weight(wg_ref, copies[1]), preferred_element_type=F32)


def _proj_b(x, mod, rows_per_seq, norm_g, w, layer, tm, tn=512):
    rows, d = x.shape
    w_args, w_specs, copy_specs, copy_shapes = _weight_operands(w, layer, 2, tn)
    width = w_args[0].shape[-1] // (1 if layer is None else 2)
    shift_spec, scale_spec, _ = _mod_specs(mod, tm, d, rows_per_seq)
    o_spec = pl.BlockSpec((tm, tn), lambda i, j: (i, j))
    f32_out = jax.ShapeDtypeStruct((rows, width), F32)
    return pl.pallas_call(
        functools.partial(_proj_b_kernel, emit_weights=layer is not None),
        grid=(rows // tm, width // tn),
        in_specs=[pl.BlockSpec((tm, d), lambda i, j: (i, 0)), shift_spec, scale_spec,
                  pl.BlockSpec((1, d), lambda i, j: (0, 0)), *w_specs],
        out_specs=[o_spec, o_spec, *copy_specs],
        out_shape=[f32_out, f32_out, *copy_shapes],
        scratch_shapes=[pltpu.VMEM((tm, d), BF16)],
        compiler_params=_params("parallel", "arbitrary"),
        name="proj_b",
    )(x, mod, mod, norm_g, *w_args)


def _out_kernel(inp_ref, w_ref, x_ref, gmod_ref, o_ref, *copy):
    w = _mxu_weight(w_ref, copy[0] if copy else None)
    y = jnp.dot(inp_ref[...].astype(BF16), w, preferred_element_type=F32)
    o_ref[...] = x_ref[...] + gmod_ref[...] * y


def _out_proj(inp, w, layer, x, mod, rows_per_seq, tm, tn):
    rows, d = x.shape
    kdim = inp.shape[1]
    w_args, w_specs, copy_specs, copy_shapes = _weight_operands(w, layer, 1, tn)
    if rows_per_seq is None:
        gmod_spec = pl.BlockSpec((tm, tn), lambda i, j: (i, 2 * (d // tn) + j))
    else:
        tiles = rows_per_seq // tm
        gmod_spec = pl.BlockSpec((None, 1, tn), lambda i, j: (i // tiles, 0, 2 * (d // tn) + j))
    o_spec = pl.BlockSpec((tm, tn), lambda i, j: (i, j))
    outs = pl.pallas_call(
        _out_kernel,
        grid=(rows // tm, d // tn),
        in_specs=[pl.BlockSpec((tm, kdim), lambda i, j: (i, 0)), *w_specs, o_spec, gmod_spec],
        out_specs=[o_spec, *copy_specs],
        out_shape=[jax.ShapeDtypeStruct((rows, d), F32), *copy_shapes],
        compiler_params=_params("parallel", "parallel"),
        name="out_proj",
    )(inp, *w_args, x, mod)
    return outs if copy_shapes else outs[0]


def _sb_block(z, tri, mask, carry):
    log_beta, l, tot = _sb_survival(z, mask)
    later = jnp.dot(l, tri, preferred_element_type=F32)
    return _sb_weights(log_beta, later, carry, mask), tot


def _sb_survival(z, mask):
    sp = jnp.maximum(jnp.log2(1.0 + jnp.exp2(jnp.minimum(z, SOFTPLUS_CLAMP))), z)
    l = sp if mask is None else jnp.where(mask, sp, 0.0)
    return z - sp, l.astype(BF16), jnp.sum(l, axis=1, keepdims=True)


def _sb_weights(log_beta, later, carry, mask):
    logp = log_beta - later
    p = jnp.exp2(logp if carry is None else logp - carry)
    return p if mask is None else jnp.where(mask, p, 0.0)


def _attn_kernel(pt_ref, qi_ref, kj_ref,
                 bias_ref, q_ref, k_ref, v_ref, gate_ref, tri_ref,
                 qs_ref, kn_ref, vn_ref, gs_ref, bias_rows_ref, tri_s_ref, *rest,
                 n_heads, steps_per_group, sample_steps):
    del pt_ref
    n_parts = PAGES_PER_STEP * (n_heads // HEAD_GROUP)
    k_pages, v_pages = rest[:n_parts], rest[n_parts:2 * n_parts]
    (o_ref, os_ref, acc_ref, carry_ref,
     qbd_ref, knew_ref, vnew_ref, accs_ref, carrys_ref) = rest[2 * n_parts:]
    step = pl.program_id(0)
    t = step % steps_per_group
    head0 = (step // steps_per_group) % (n_heads // HEADS_PER_STEP) * HEADS_PER_STEP
    qi, kj = qi_ref[step], kj_ref[step]
    n_tok = qs_ref.shape[0]
    n_chunks = n_heads // 2
    cw = 2 * HEAD_DIM
    rows = n_heads * n_tok
    heads = [slice(hh * HEAD_DIM, (hh + 1) * HEAD_DIM) for hh in range(HEADS_PER_STEP)]

    def new_chunk(ref, c):
        return ref[:, c * cw:(c + 1) * cw].astype(BF16)

    def page_chunk(parts, c):
        n_groups = n_heads // HEAD_GROUP
        grp, local = divmod(2 * c, HEAD_GROUP)

        def head(r, hd):
            flat = r.reshape(PAGE * HEAD_GROUP, HEAD_DIM)
            return flat[pl.ds(hd, PAGE, stride=HEAD_GROUP), :]
        return jnp.concatenate(
            [jnp.concatenate([head(r, local), head(r, local + 1)], axis=1)
             for r in parts[grp::n_groups]], axis=0).astype(BF16)

    def sample_scores(load_k, c):
        return lax.dot_general(qbd_ref[c], load_k(c), (((1,), (1,)), ((), ())),
                               preferred_element_type=F32)

    def sample_weights(z_chunks, width, mask):
        z = jnp.concatenate(z_chunks, axis=0) + bias_rows_ref[...] * LOG2E
        carry = carrys_ref[...]
        p, tot = _sb_block(z, tri_s_ref[:width, :width], mask, carry[:, :1])
        carrys_ref[...] = carry + tot
        return p.astype(BF16)

    def sample_values(p, load_v, c):
        rs = slice(c * 2 * n_tok, (c + 1) * 2 * n_tok)
        accs_ref[rs, :] += jnp.dot(p[rs], load_v(c), preferred_element_type=F32)

    @pl.when(t == 0)
    def _():
        accs_ref[...] = jnp.zeros_like(accs_ref)
        carrys_ref[...] = jnp.zeros_like(carrys_ref)
        zero = jnp.zeros((n_tok, HEAD_DIM), F32)
        for c in range(n_chunks):
            q0 = qs_ref[:, (2 * c) * HEAD_DIM:(2 * c + 1) * HEAD_DIM]
            q1 = qs_ref[:, (2 * c + 1) * HEAD_DIM:(2 * c + 2) * HEAD_DIM]
            top = jnp.concatenate([q0, zero], axis=1)
            bot = jnp.concatenate([zero, q1], axis=1)
            qbd_ref[c] = jnp.concatenate([top, bot], axis=0).astype(BF16)
        knew_ref[...] = jnp.zeros_like(knew_ref)
        vnew_ref[...] = jnp.zeros_like(vnew_ref)
        knew_ref[:n_tok, :] = kn_ref[...]
        vnew_ref[:n_tok, :] = vn_ref[...]
        tok = lax.broadcasted_iota(jnp.int32, (rows, PAGE), 0) % n_tok
        col = lax.broadcasted_iota(jnp.int32, (rows, PAGE), 1)
        load_k = functools.partial(new_chunk, knew_ref)
        load_v = functools.partial(new_chunk, vnew_ref)
        p = sample_weights([sample_scores(load_k, c) for c in range(n_chunks)], PAGE, col < tok)
        for c in range(n_chunks):
            sample_values(p, load_v, c)

    @pl.when(kj == qi)
    def _():
        acc_ref[...] = jnp.zeros_like(acc_ref)
        carry_ref[...] = jnp.zeros_like(carry_ref)

    def block(with_sample, diagonal):
        hs = range(HEADS_PER_STEP)
        tri = tri_ref[...]
        if diagonal:
            row = lax.broadcasted_iota(jnp.int32, (Q_TILE, K_TILE), 0)
            col = lax.broadcasted_iota(jnp.int32, (Q_TILE, K_TILE), 1)
            keep = col < row
        st = [dict() for _ in hs]
        load_k = functools.partial(page_chunk, k_pages)
        load_v = functools.partial(page_chunk, v_pages)
        zs = []

        def qk(hh):
            z = lax.dot_general(q_ref[:, heads[hh]], k_ref[:, heads[hh]], (((1,), (1,)), ((), ())),
                                preferred_element_type=F32)
            z = z + bias_ref[head0 + hh] * LOG2E
            st[hh]["z"] = jnp.where(keep, z, -MASKED_LOGIT) if diagonal else z
            if with_sample:
                zs.append(sample_scores(load_k, hh))

        def survive(hh):
            s = st[hh]
            s["log_beta"], l, s["tot"] = _sb_survival(s.pop("z"), None)
            s["later"] = jnp.dot(l, tri, preferred_element_type=F32)

        def weigh(hh):
            s = st[hh]
            s["carry"] = carry_ref[hh]
            carry = jnp.concatenate([s["carry"]] * (K_TILE // HEAD_DIM), axis=1)
            p = _sb_weights(s.pop("log_beta"), s.pop("later"), carry, None)
            s["pv"] = jnp.dot(p.astype(BF16), v_ref[:, heads[hh]], preferred_element_type=F32)

        def accumulate(hh):
            s = st[hh]
            acc_ref[hh] += s["pv"]
            carry_ref[hh] = s["carry"] + s["tot"]
            if with_sample:
                sample_values(st[0]["ps"], load_v, hh)

        for hh in hs:
            qk(hh)
        for hh in hs:
            survive(hh)
        if with_sample:
            st[0]["ps"] = sample_weights(zs, PAGES_PER_STEP * PAGE, None)
        for hh in hs:
            weigh(hh)
        for hh in hs:
            accumulate(hh)

    for with_sample in (True, False):
        for diagonal in (True, False):
            pl.when(((t < sample_steps) == with_sample) & ((kj == qi) == diagonal))(
                functools.partial(block, with_sample, diagonal))

    @pl.when(kj == 0)
    def _():
        for hh in range(HEADS_PER_STEP):
            o_ref[:, heads[hh]] = (acc_ref[hh] * _silu(gate_ref[:, heads[hh]])).astype(o_ref.dtype)

    @pl.when(t == sample_steps - 1)
    def _():
        for hd in range(n_heads):
            c, hh = hd // 2, hd % 2
            r0 = (c * 2 + hh) * n_tok
            o = accs_ref[r0:r0 + n_tok, hh * HEAD_DIM:(hh + 1) * HEAD_DIM]
            sl = slice(hd * HEAD_DIM, (hd + 1) * HEAD_DIM)
            os_ref[:, sl] = o * _silu(gs_ref[:, sl])


def _attention(q, k, v, gate, bias, tri, batch, seq,
               page_table, q_s, k_new, v_new, gate_s, bias_rows, tri_s, cache_k, cache_v, layer):
    rows, width = q.shape
    n_heads = width // HEAD_DIM
    nq = seq // Q_TILE
    head_groups = n_heads // HEADS_PER_STEP
    dec_batch, n_pages = page_table.shape
    rows_s = q_s.shape[0]
    n_tok = rows_s // dec_batch
    npg = PAGES_PER_STEP
    sample_steps = n_pages // npg
    walk = [(i, j) for i in range(nq) for j in range(i, -1, -1)]
    steps_per_group = len(walk)
    n_groups = batch * head_groups
    assert n_groups == dec_batch and sample_steps <= steps_per_group
    assert K_TILE == Q_TILE and HEADS_PER_STEP * 2 == n_heads
    qi_tab = jnp.asarray([w[0] for w in walk] * n_groups, jnp.int32)
    kj_tab = jnp.asarray([w[1] for w in walk] * n_groups, jnp.int32)
    hw = HEADS_PER_STEP * HEAD_DIM

    def group(s):
        g = s // steps_per_group
        return g // head_groups, g % head_groups

    def q_index(s, pt, qi, kj):
        b, hg = group(s)
        return (b * nq + qi[s], hg)

    def k_index(s, pt, qi, kj):
        b, hg = group(s)
        return (b * nq + kj[s], hg)

    q_spec = pl.BlockSpec((Q_TILE, hw), q_index)
    k_spec = pl.BlockSpec((K_TILE, hw), k_index)
    tok_spec = pl.BlockSpec((n_tok, width), lambda s, pt, qi, kj: (s // steps_per_group, 0))

    def page_spec(p, grp):
        def index(s, pt, qi, kj):
            g = jnp.minimum(s % steps_per_group, sample_steps - 1)
            return (layer, pt[s // steps_per_group, (sample_steps - 1 - g) * npg + p], 0, grp, 0)
        return pl.BlockSpec((None, None, PAGE, HEAD_GROUP, HEAD_DIM), index)

    page_specs = [page_spec(p, grp) for p in range(npg) for grp in range(n_heads // HEAD_GROUP)]
    const = lambda s, pt, qi, kj: (0, 0)
    acc = pltpu.VMEM((HEADS_PER_STEP, Q_TILE, HEAD_DIM), F32)
    grid_spec = pltpu.PrefetchScalarGridSpec(
        num_scalar_prefetch=3,
        grid=(n_groups * steps_per_group,),
        in_specs=[pl.BlockSpec(memory_space=pltpu.SMEM), q_spec, k_spec, k_spec, q_spec,
                  pl.BlockSpec((K_TILE, K_TILE), const),
                  tok_spec, tok_spec, tok_spec, tok_spec,
                  pl.BlockSpec((n_heads * n_tok, 1), const),
                  pl.BlockSpec((npg * PAGE, npg * PAGE), const)]
                 + page_specs + page_specs,
        out_specs=[q_spec, tok_spec],
        scratch_shapes=[acc, acc,
                        pltpu.VMEM((n_heads // 2, 2 * n_tok, 2 * HEAD_DIM), BF16),
                        pltpu.VMEM((PAGE, width), F32),
                        pltpu.VMEM((PAGE, width), F32),
                        pltpu.VMEM((n_heads * n_tok, 2 * HEAD_DIM), F32),
                        pltpu.VMEM((n_heads * n_tok, HEAD_DIM), F32)],
    )
    return pl.pallas_call(
        functools.partial(_attn_kernel, n_heads=n_heads, steps_per_group=steps_per_group,
                          sample_steps=sample_steps),
        grid_spec=grid_spec,
        out_shape=[jax.ShapeDtypeStruct((rows, width), BF16),
                   jax.ShapeDtypeStruct((rows_s, width), F32)],
        compiler_params=_params("arbitrary"),
        name="attention",
    )(page_table, qi_tab, kj_tab, bias, q, k, v, gate, tri,
      q_s, k_new, v_new, gate_s, bias_rows, tri_s,
      *([cache_k] * len(page_specs)), *([cache_v] * len(page_specs)))


def _group_mix(mean, u, gate, w_grp, scale):
    diff = (mean - u).astype(BF16)
    return jnp.dot(diff, w_grp, preferred_element_type=F32) * scale * _silu(gate)


def _pool_prompt_kernel(u_ref, halo_ref, gate_ref, wgrp_ref, scale_ref, band_ref, o_ref,
                        hi_ref, lo_ref, *, tiles_per_seq):
    tm = u_ref.shape[0]
    gdim = wgrp_ref.shape[1]
    tile = pl.program_id(0) % tiles_per_seq
    halo = jnp.where(tile == 0, 0.0, halo_ref[...])
    hi, lo = _split_bf16(halo)
    hi_ref[:POOL_SUB, :] = hi
    lo_ref[:POOL_SUB, :] = lo
    hi, lo = _split_bf16(u_ref[...])
    hi_ref[POOL_SUB:, :] = hi
    lo_ref[POOL_SUB:, :] = lo
    for m in range(tm // POOL_SUB):
        r0 = m * POOL_SUB
        pos = tile * tm + r0 + lax.broadcasted_iota(jnp.int32, (POOL_SUB, 1), 0)
        for g, w in enumerate(POOL_WINDOWS):
            cs = slice(g * gdim, (g + 1) * gdim)
            band = band_ref[g]
            total = (jnp.dot(band, hi_ref[r0:r0 + 2 * POOL_SUB, cs], preferred_element_type=F32)
                     + jnp.dot(band, lo_ref[r0:r0 + 2 * POOL_SUB, cs], preferred_element_type=F32))
            inv_cnt = 1.0 / jnp.minimum(w, pos + 1).astype(F32)
            mixed = _group_mix(total * inv_cnt, u_ref[r0:r0 + POOL_SUB, cs],
                               gate_ref[r0:r0 + POOL_SUB, cs], wgrp_ref[g], scale_ref[:, cs])
            o_ref[r0:r0 + POOL_SUB, cs] = mixed.astype(o_ref.dtype)


def _pool_prompt(u, gate, w_grp, scale, bands, seq, tm=512):
    rows, width = u.shape
    sub_per_tile = tm // POOL_SUB
    tile_spec = pl.BlockSpec((tm, width), lambda i: (i, 0))
    return pl.pallas_call(
        functools.partial(_pool_prompt_kernel, tiles_per_seq=seq // tm),
        grid=(rows // tm,),
        in_specs=[tile_spec,
                  pl.BlockSpec((POOL_SUB, width), lambda i: (jnp.maximum(i * sub_per_tile - 1, 0), 0)),
                  tile_spec,
                  pl.BlockSpec(w_grp.shape, lambda i: (0, 0, 0)),
                  pl.BlockSpec((1, width), lambda i: (0, 0)),
                  pl.BlockSpec(bands.shape, lambda i: (0, 0, 0))],
        out_specs=tile_spec,
        out_shape=jax.ShapeDtypeStruct((rows, width), BF16),
        scratch_shapes=[pltpu.VMEM((tm + POOL_SUB, width), BF16),
                        pltpu.VMEM((tm + POOL_SUB, width), BF16)],
        compiler_params=_params("parallel"),
        name="pool_prompt",
    )(u, u, gate, w_grp, scale, bands)


def _pool_sample_kernel(ext_ref, gate_ref, wgrp_ref, scale_ref, o_ref):
    n_tok = gate_ref.shape[0]
    gdim = wgrp_ref.shape[1]
    last = ext_ref.shape[0] - n_tok
    for g, w in enumerate(POOL_WINDOWS):
        cs = slice(g * gdim, (g + 1) * gdim)
        u = ext_ref[last:last + n_tok, cs]
        total = u
        for back in range(1, w):
            total = total + ext_ref[last - back:last - back + n_tok, cs]
        o_ref[:, cs] = _group_mix(total / float(w), u, gate_ref[:, cs], wgrp_ref[g], scale_ref[:, cs])


def _pool_sample(ext, gate, w_grp, scale):
    batch, ext_rows, width = ext.shape
    rows = gate.shape[0]
    n_tok = rows // batch
    tok_spec = pl.BlockSpec((n_tok, width), lambda b: (b, 0))
    return pl.pallas_call(
        _pool_sample_kernel,
        grid=(batch,),
        in_specs=[pl.BlockSpec((None, ext_rows, width), lambda b: (b, 0, 0)),
                  tok_spec,
                  pl.BlockSpec(w_grp.shape, lambda b: (0, 0, 0)),
                  pl.BlockSpec((1, width), lambda b: (0, 0))],
        out_specs=tok_spec,
        out_shape=jax.ShapeDtypeStruct((rows, width), F32),
        compiler_params=_params("parallel"),
        name="pool_sample",
    )(ext, gate, w_grp, scale)


def _strict_lower(n):
    j = lax.broadcasted_iota(jnp.int32, (n, n), 0)
    s = lax.broadcasted_iota(jnp.int32, (n, n), 1)
    return (j > s).astype(BF16)


def _pool_bands():
    r = lax.broadcasted_iota(jnp.int32, (POOL_SUB, 2 * POOL_SUB), 0)
    c = lax.broadcasted_iota(jnp.int32, (POOL_SUB, 2 * POOL_SUB), 1)
    back = POOL_SUB + r - c
    return jnp.stack([((back >= 0) & (back < w)).astype(BF16) for w in POOL_WINDOWS])


def kernel(x_prompt, x_sample, cache_k, cache_v, state_pool, page_table, c_prompt, c_sample,
           norm_g, w_mod, b_mod, w_in_a, q_norm_g, k_norm_g, sb_bias, w_out_a,
           w_in_b, w_grp_b, scale_b, w_out_b):
    batch, seq, d = x_prompt.shape
    dec_batch, dec_seq, _ = x_sample.shape
    depth = norm_g.shape[0]
    n_heads = sb_bias.shape[1]
    width_a = n_heads * HEAD_DIM
    n_phys = cache_k.shape[1]
    rows_p, rows_s = batch * seq, dec_batch * dec_seq
    assert cache_k.shape[2] == PAGE and cache_k.shape[4] == HEAD_DIM
    assert state_pool.shape[2] == POOL_HIST and seq % 512 == 0

    n_seq = batch + dec_batch
    pad = (-n_seq) % 8
    c_all = jnp.concatenate([c_prompt, c_sample, jnp.zeros((pad, d), F32)], axis=0)
    mods = _modulation(c_all, w_mod, b_mod)

    tri_p = _strict_lower(K_TILE)
    tri_s = _strict_lower(PAGES_PER_STEP * PAGE)
    bands = _pool_bands()

    xp = x_prompt.reshape(rows_p, d)
    xs = x_sample.reshape(rows_s, d)
    nk_s, nv_s, npool_p, npool_s = [], [], [], []
    n_a = (depth + 1) // 2
    kv_stack = tuple(jnp.zeros((n_a, rows_p, n_heads, HEAD_DIM), F32) for _ in range(2))
    for i in range(depth):
        mod_p = mods[i, :batch].reshape(batch, 1, 3 * d)
        mod_s = jnp.repeat(mods[i, batch:n_seq], dec_seq, axis=0)
        g_i = norm_g[i].reshape(1, d)
        j = i // 2
        if i % 2 == 0:
            w_out = w_out_a
            gq = q_norm_g[j].reshape(1, HEAD_DIM)
            gk = k_norm_g[j].reshape(1, HEAD_DIM)
            q_s, kf_s, _, vf_s, _, gate_s, *w4 = _proj_a(
                xs, mod_s, None, g_i, w_in_a, j, gq, gk, F32, tm=rows_s)
            q_p, k_all, kb_p, v_all, vb_p, gate_p = _proj_a(
                xp, mod_p, seq, g_i, w4, None, gq, gk, BF16, tm=512,
                kv_slab=(j, n_a), kv_stack=kv_stack)
            kv_stack = (k_all, v_all)
            bias_rows = jnp.repeat(sb_bias[j], dec_seq).reshape(n_heads * dec_seq, 1)
            o_p, o_s = _attention(q_p, kb_p, vb_p, gate_p, sb_bias[j], tri_p, batch, seq,
                                  page_table, q_s, kf_s, vf_s, gate_s, bias_rows, tri_s,
                                  cache_k, cache_v, j)
            nk_s.append(kf_s.reshape(dec_batch, dec_seq, n_heads, HEAD_DIM))
            nv_s.append(vf_s.reshape(dec_batch, dec_seq, n_heads, HEAD_DIM))
        else:
            width_b = w_in_b.shape[2] // 2
            w_out = w_out_b
            w_grp = w_grp_b[j].astype(BF16)
            scale = scale_b[j].reshape(1, width_b)
            u_s, gate_s, *w2 = _proj_b(xs, mod_s, None, g_i, w_in_b, j, tm=rows_s)
            u_p, gate_p = _proj_b(xp, mod_p, seq, g_i, w2, None, tm=1024)
            o_p = _pool_prompt(u_p, gate_p, w_grp, scale, bands, seq)
            ext = jnp.concatenate([state_pool[j], u_s.reshape(dec_batch, dec_seq, width_b)], axis=1)
            front = (-POOL_HIST) % 8
            ext_pad = jnp.pad(ext, ((0, 0), (front, 0), (0, 0)))
            o_s = _pool_sample(ext_pad, gate_s, w_grp, scale)
            npool_p.append(u_p.reshape(batch, seq, width_b)[:, seq - POOL_HIST:])
            npool_s.append(ext[:, dec_seq:])
        xs, w_out_bf = _out_proj(o_s, w_out, j, xs, mod_s, None, tm=rows_s, tn=d // 2)
        xp = _out_proj(o_p, [w_out_bf], None, xp, mod_p, seq, tm=512, tn=d)
    nk_p = kv_stack[0].reshape(n_a, batch, seq, n_heads, HEAD_DIM)
    nv_p = kv_stack[1].reshape(n_a, batch, seq, n_heads, HEAD_DIM)
    return (xp.reshape(batch, seq, d), xs.reshape(dec_batch, dec_seq, d),
            nk_p, nv_p, jnp.stack(nk_s), jnp.stack(nv_s),
            jnp.stack(npool_p), jnp.stack(npool_s))
```
